```python
import math
import jax
import jax.numpy as jnp
from jax import lax
import numpy as np

D_MODEL = 2048
BATCH = 4
SEQ = 2048
DEPTH = 1
DEC_BATCH = 32
DEC_SEQ = 4
PAST_LEN = 8192
PAGE_SIZE = 128

HEAD_DIM = 128
N_HEADS = D_MODEL // HEAD_DIM
H_SB = N_HEADS // 2
H_NSA = N_HEADS - H_SB
G_NSA = 2
HPG = H_NSA // G_NSA
SB_W = H_SB * HEAD_DIM
NSA_W = H_NSA * HEAD_DIM
KV_W = 2 * G_NSA * HEAD_DIM
D_IN = 3 * SB_W + NSA_W + 3 * KV_W + 3 * H_NSA
D_MIX = SB_W + NSA_W
SCALE = HEAD_DIM ** -0.5
ROT_DIM = HEAD_DIM // 4
ROPE_THETA = 500000.0
CMP_BLK = 64
TOP_BLOCKS = 16
WINDOW = 512
Q_BLK = 128
FORCE_SCORE = 1e9
N_KEYS = 128
N_EXPERTS = N_KEYS * N_KEYS
PEER_HEADS = 8
PEER_TOPK = 16
D_QUERY = 256
PEER_CHUNK = 128
EPS = 1e-6
NEG_INF = -1e30

kernel_name = 'hymba_sb_nsa_peer_adaln_step'


def rmsnorm(x, g):
    xf = x.astype(jnp.float32)
    y = xf * lax.rsqrt(jnp.mean(xf * xf, axis=-1, keepdims=True) + EPS)
    return (y * g.astype(jnp.float32)).astype(x.dtype)


def masked_softmax(s, mask, axis=-1):
    s = jnp.where(mask, s, NEG_INF)
    return jnp.where(mask, jax.nn.softmax(s, axis=axis), 0.0)


def rope(x, pos):
    half = ROT_DIM // 2
    inv = jnp.power(ROPE_THETA, -jnp.arange(half, dtype=jnp.float32) / half)
    ang = pos.astype(jnp.float32)[:, None] * inv[None, :]
    cos = jnp.cos(ang)[:, None, :].astype(x.dtype)
    sin = jnp.sin(ang)[:, None, :].astype(x.dtype)
    x1, x2 = x[..., :half], x[..., half:ROT_DIM]
    return jnp.concatenate([x1 * cos - x2 * sin, x2 * cos + x1 * sin, x[..., ROT_DIM:]], axis=-1)


def project(h, w_in, pos):
    b, t, _ = h.shape
    p = h @ w_in
    q_sb = p[..., 0:SB_W].reshape(b, t, H_SB, HEAD_DIM)
    k_sb = p[..., SB_W:2 * SB_W].reshape(b, t, H_SB, HEAD_DIM)
    v_sb = p[..., 2 * SB_W:3 * SB_W].reshape(b, t, H_SB, HEAD_DIM)
    off = 3 * SB_W
    q_nsa = rope(p[..., off:off + NSA_W].reshape(b, t, H_NSA, HEAD_DIM), pos)
    off += NSA_W
    kvs = []
    for _ in range(3):
        kv = p[..., off:off + KV_W].reshape(b, t, 2, G_NSA, HEAD_DIM)
        kvs.append(jnp.stack([rope(kv[:, :, 0], pos), kv[:, :, 1]], axis=2))
        off += KV_W
    gates = jax.nn.sigmoid(p[..., off:].reshape(b, t, H_NSA, 3))
    kv_sb = jnp.stack([k_sb, v_sb], axis=2)
    return q_sb, kv_sb, q_nsa, kvs[0], kvs[1], kvs[2], gates


def sb_attend(q, qpos, k, v, kpos):
    z = jnp.einsum('bqhd,bkhd->bhqk', q, k).astype(jnp.float32) * SCALE
    valid = kpos[None, :] < qpos[:, None]
    log1m = jnp.where(valid, jax.nn.log_sigmoid(-z), 0.0)
    after = lax.cumsum(log1m, axis=3, reverse=True) - log1m
    a = jnp.where(valid, jnp.exp(jax.nn.log_sigmoid(z) + after), 0.0)
    return jnp.einsum('bhqk,bkhd->bqhd', a.astype(v.dtype), v)


def to_blocks(rows):
    b, t = rows.shape[:2]
    nb = -(-t // CMP_BLK)
    rows = jnp.pad(rows, ((0, 0), (0, nb * CMP_BLK - t), (0, 0), (0, 0)))
    return rows.reshape(b, nb, CMP_BLK, G_NSA, HEAD_DIM).transpose(0, 3, 1, 2, 4)


def compress(rows, pe, w1, w2):
    blk = to_blocks(rows) + pe[None, None, None]
    hid = jax.nn.gelu(jnp.einsum('bgnpd,pde->bgne', blk, w1))
    return jnp.einsum('bgne,ef->bgnf', hid, w2)


def cmp_select(q, qpos, kv_rows, W):
    b, nq = q.shape[:2]
    kc = compress(kv_rows[:, :, 0], W['cmp_pe_k'], W['cmp_w1_k'], W['cmp_w2_k'])
    vc = compress(kv_rows[:, :, 1], W['cmp_pe_v'], W['cmp_w1_v'], W['cmp_w2_v'])
    nb = kc.shape[2]
    blk = jnp.arange(nb)
    complete = (blk[None, :] + 1) * CMP_BLK <= qpos[:, None] + 1
    qg = q.reshape(b, nq, G_NSA, HPG, HEAD_DIM)
    s = jnp.einsum('bqgnd,bgkd->bqgnk', qg, kc).astype(jnp.float32) * SCALE
    p = masked_softmax(s, complete[None, :, None, None, :])
    o = jnp.einsum('bqgnk,bgkd->bqgnd', p.astype(vc.dtype), vc).reshape(b, nq, H_NSA, HEAD_DIM)
    cur = qpos[:, None] // CMP_BLK
    started = blk[None, :] <= cur
    forced = started & ((blk[None, :] == 0) | (blk[None, :] >= cur - 1))
    imp = jnp.sum(p, axis=3)
    score = jnp.where(forced[None, :, None, :], FORCE_SCORE,
                      jnp.where(started[None, :, None, :], imp, -FORCE_SCORE))
    _, idx = lax.top_k(score, min(TOP_BLOCKS, nb))
    return o, idx


def sel_attend(q, qpos, idx, kb, vb):
    b, nq = q.shape[:2]
    bi = jnp.arange(b)[:, None, None, None]
    gi = jnp.arange(G_NSA)[None, None, :, None]
    ks = kb[bi, gi, idx]
    vs = vb[bi, gi, idx]
    kpos = idx[..., None] * CMP_BLK + jnp.arange(CMP_BLK)
    mask = kpos <= qpos[None, :, None, None, None]
    qg = q.reshape(b, nq, G_NSA, HPG, HEAD_DIM)
    s = jnp.einsum('bqgnd,bqgkpd->bqgnkp', qg, ks).astype(jnp.float32) * SCALE
    p = masked_softmax(s, mask[:, :, :, None], axis=(4, 5))
    o = jnp.einsum('bqgnkp,bqgkpd->bqgnd', p.astype(vs.dtype), vs)
    return o.reshape(b, nq, H_NSA, HEAD_DIM)


def win_attend(q, qpos, k, v, kpos):
    b, nq = q.shape[:2]
    d = qpos[:, None] - kpos[None, :]
    mask = (d >= 0) & (d < WINDOW) & (kpos[None, :] >= 0)
    qg = q.reshape(b, nq, G_NSA, HPG, HEAD_DIM)
    s = jnp.einsum('bqgnd,blgd->bqgnl', qg, k).astype(jnp.float32) * SCALE
    p = masked_softmax(s, mask[None, :, None, None, :])
    o = jnp.einsum('bqgnl,blgd->bqgnd', p.astype(v.dtype), v)
    return o.reshape(b, nq, H_NSA, HEAD_DIM)


def peer_ffn(h, wq, k1, k2, u, v):
    b, t, d = h.shape
    hf = h.reshape(-1, d)
    n = hf.shape[0]
    q = (hf @ wq).reshape(n, PEER_HEADS, 2, D_QUERY // 2)
    s1 = jnp.einsum('nhd,hkd->nhk', q[:, :, 0], k1).astype(jnp.float32)
    s2 = jnp.einsum('nhd,hkd->nhk', q[:, :, 1], k2).astype(jnp.float32)
    v1, i1 = lax.top_k(s1, PEER_TOPK)
    v2, i2 = lax.top_k(s2, PEER_TOPK)
    cand = (v1[..., :, None] + v2[..., None, :]).reshape(n, PEER_HEADS, PEER_TOPK * PEER_TOPK)
    vals, pos = lax.top_k(cand, PEER_TOPK)
    e = (jnp.take_along_axis(i1, pos // PEER_TOPK, axis=-1) * N_KEYS
         + jnp.take_along_axis(i2, pos % PEER_TOPK, axis=-1))
    g = jax.nn.softmax(vals, axis=-1)
    e = e.reshape(n, PEER_HEADS * PEER_TOPK)
    g = g.reshape(n, PEER_HEADS * PEER_TOPK).astype(h.dtype)
    pad = (-n) % PEER_CHUNK
    hf = jnp.pad(hf, ((0, pad), (0, 0)))
    e = jnp.pad(e, ((0, pad), (0, 0)))
    g = jnp.pad(g, ((0, pad), (0, 0)))
    nc = (n + pad) // PEER_CHUNK

    def chunk(args):
        hc, ec, gc = args
        a = jax.nn.gelu(jnp.einsum('cd,ced->ce', hc, u[ec]))
        return jnp.einsum('ce,ced->cd', gc * a, v[ec])

    out = lax.map(chunk, (hf.reshape(nc, PEER_CHUNK, d),
                          e.reshape(nc, PEER_CHUNK, -1),
                          g.reshape(nc, PEER_CHUNK, -1)))
    return out.reshape(-1, d)[:n].reshape(b, t, d)


def prompt_mix(h, W):
    b, s, _ = h.shape
    pos = jnp.arange(s)
    q_sb, kv_sb, q_nsa, kv_cmp, kv_slc, kv_win, gates = project(h, W['w_in'], pos)
    k_sb, v_sb = kv_sb[:, :, 0], kv_sb[:, :, 1]
    o_cmp, idx = cmp_select(q_nsa, pos, kv_cmp, W)
    kb, vb = to_blocks(kv_slc[:, :, 0]), to_blocks(kv_slc[:, :, 1])
    win_pad = jnp.pad(kv_win, ((0, 0), (WINDOW, 0), (0, 0), (0, 0), (0, 0)))

    def sweep(i):
        s0 = i * Q_BLK
        qpos = s0 + jnp.arange(Q_BLK)
        qs = lax.dynamic_slice_in_dim(q_sb, s0, Q_BLK, axis=1)
        qn = lax.dynamic_slice_in_dim(q_nsa, s0, Q_BLK, axis=1)
        ib = lax.dynamic_slice_in_dim(idx, s0, Q_BLK, axis=1)
        slab = lax.dynamic_slice_in_dim(win_pad, s0, WINDOW + Q_BLK, axis=1)
        o_sb = sb_attend(qs, qpos, k_sb, v_sb, pos)
        o_slc = sel_attend(qn, qpos, ib, kb, vb)
        o_win = win_attend(qn, qpos, slab[:, :, 0], slab[:, :, 1],
                           s0 - WINDOW + jnp.arange(WINDOW + Q_BLK))
        return o_sb, o_slc, o_win

    o_sb, o_slc, o_win = lax.map(sweep, jnp.arange(s // Q_BLK))
    unblock = lambda o: jnp.moveaxis(o, 0, 1).reshape(b, s, o.shape[3], HEAD_DIM)
    o_nsa = (gates[..., 0:1] * o_cmp + gates[..., 1:2] * unblock(o_slc)
             + gates[..., 2:3] * unblock(o_win))
    win_state = kv_win[:, s - min(WINDOW, s):]
    return unblock(o_sb), o_nsa, (kv_sb, kv_cmp, kv_slc, win_state)


def sample_mix(h, c_sb, c_cmp, c_slc, st_win, page_table, W):
    b, nq, _ = h.shape
    past = page_table.shape[1] * c_sb.shape[1]
    qpos = past + jnp.arange(nq)
    kpos = jnp.arange(past + nq)
    q_sb, kv_sb, q_nsa, kv_cmp, kv_slc, kv_win, gates = project(h, W['w_in'], qpos)

    def gather(cache):
        return cache[page_table].reshape(b, past, *cache.shape[2:])

    def sb_one(args):
        pt, q1, kv1 = args
        rows = jnp.concatenate([c_sb[pt].reshape(past, *c_sb.shape[2:]), kv1], axis=0)
        return sb_attend(q1[None], qpos, rows[None, :, 0], rows[None, :, 1], kpos)[0]

    o_sb = lax.map(sb_one, (page_table, q_sb, kv_sb))
    full_cmp = jnp.concatenate([gather(c_cmp), kv_cmp], axis=1)
    full_slc = jnp.concatenate([gather(c_slc), kv_slc], axis=1)
    o_cmp, idx = cmp_select(q_nsa, qpos, full_cmp, W)
    o_slc = sel_attend(q_nsa, qpos, idx, to_blocks(full_slc[:, :, 0]), to_blocks(full_slc[:, :, 1]))
    wb = st_win.shape[1]
    slab = jnp.concatenate([st_win, kv_win], axis=1)
    o_win = win_attend(q_nsa, qpos, slab[:, :, 0], slab[:, :, 1], past - wb + jnp.arange(wb + nq))
    o_nsa = gates[..., 0:1] * o_cmp + gates[..., 1:2] * o_slc + gates[..., 2:3] * o_win
    return o_sb, o_nsa, (kv_sb, kv_cmp, kv_slc, slab[:, nq:])


def trunk_block(x, c, mix_fn, W):
    b, t, d = x.shape
    mod = (c @ W['w_ada'] + W['b_ada']).reshape(b, 6, 1, d)
    h = rmsnorm(x, W['norm1_g']) * (1.0 + mod[:, 1]) + mod[:, 0]
    o_sb, o_nsa, new_state = mix_fn(h)
    mixed = jnp.concatenate([rmsnorm(o_sb.reshape(b, t, SB_W), W['out_g_sb']),
                             rmsnorm(o_nsa.reshape(b, t, NSA_W), W['out_g_nsa'])], axis=-1) @ W['w_out']
    x = x + mod[:, 2] * mixed
    h2 = rmsnorm(x, W['norm2_g']) * (1.0 + mod[:, 4]) + mod[:, 3]
    x = x + mod[:, 5] * peer_ffn(h2, W['peer_wq'], W['peer_k1'], W['peer_k2'], W['peer_u'], W['peer_v'])
    return x, new_state


def setup_inputs(seed: int = 0) -> dict:
    key = jax.random.key(seed)
    keys = iter(jax.random.split(key, 40))

    def nrm(shape, scale):
        return jax.random.normal(next(keys), shape, jnp.float32) * scale

    def gain(shape):
        return 1.0 + nrm(shape, 0.05)

    n_pages = PAST_LEN // PAGE_SIZE
    n_pool = (DEC_BATCH * n_pages * 5) // 4
    wb = min(WINDOW, PAST_LEN)
    L, D = DEPTH, D_MODEL
    page_table = jax.random.permutation(next(keys), n_pool)[:DEC_BATCH * n_pages]
    return {
        'x_prompt': nrm((BATCH, SEQ, D), 1.0),
        'x_sample': nrm((DEC_BATCH, DEC_SEQ, D), 1.0),
        'cache_sb': nrm((L, n_pool, PAGE_SIZE, 2, H_SB, HEAD_DIM), 1.0),
        'cache_cmp': nrm((L, n_pool, PAGE_SIZE, 2, G_NSA, HEAD_DIM), 1.0),
        'cache_slc': nrm((L, n_pool, PAGE_SIZE, 2, G_NSA, HEAD_DIM), 1.0),
        'state_win': nrm((L, DEC_BATCH, wb, 2, G_NSA, HEAD_DIM), 1.0),
        'page_table': page_table.reshape(DEC_BATCH, n_pages).astype(jnp.int32),
        'c_prompt': nrm((BATCH, D), 1.0),
        'c_sample': nrm((DEC_BATCH, D), 1.0),
        'w_ada': nrm((L, D, 6 * D), 0.5 * D ** -0.5),
        'b_ada': nrm((L, 6 * D), 0.02),
        'norm1_g': gain((L, D)),
        'w_in': nrm((L, D, D_IN), D ** -0.5),
        'cmp_pe_k': nrm((L, CMP_BLK, HEAD_DIM), 0.1),
        'cmp_w1_k': nrm((L, CMP_BLK, HEAD_DIM, HEAD_DIM), (CMP_BLK * HEAD_DIM) ** -0.5),
        'cmp_w2_k': nrm((L, HEAD_DIM, HEAD_DIM), HEAD_DIM ** -0.5),
        'cmp_pe_v': nrm((L, CMP_BLK, HEAD_DIM), 0.1),
        'cmp_w1_v': nrm((L, CMP_BLK, HEAD_DIM, HEAD_DIM), (CMP_BLK * HEAD_DIM) ** -0.5),
        'cmp_w2_v': nrm((L, HEAD_DIM, HEAD_DIM), HEAD_DIM ** -0.5),
        'out_g_sb': gain((L, SB_W)),
        'out_g_nsa': gain((L, NSA_W)),
        'w_out': nrm((L, D_MIX, D), D_MIX ** -0.5),
        'norm2_g': gain((L, D)),
        'peer_wq': nrm((L, D, PEER_HEADS * D_QUERY), D ** -0.5),
        'peer_k1': nrm((L, PEER_HEADS, N_KEYS, D_QUERY // 2), (D_QUERY // 2) ** -0.5),
        'peer_k2': nrm((L, PEER_HEADS, N_KEYS, D_QUERY // 2), (D_QUERY // 2) ** -0.5),
        'peer_u': nrm((L, N_EXPERTS, D), D ** -0.5),
        'peer_v': nrm((L, N_EXPERTS, D), PEER_HEADS ** -0.5),
        'final_g': gain((D,)),
    }


def reference(x_prompt, x_sample, cache_sb, cache_cmp, cache_slc, state_win, page_table,
              c_prompt, c_sample, w_ada, b_ada, norm1_g, w_in, cmp_pe_k, cmp_w1_k, cmp_w2_k,
              cmp_pe_v, cmp_w1_v, cmp_w2_v, out_g_sb, out_g_nsa, w_out, norm2_g,
              peer_wq, peer_k1, peer_k2, peer_u, peer_v, final_g):
    yp, ys = x_prompt, x_sample
    st_p, st_s = [], []
    for l in range(DEPTH):
        W = {'w_ada': w_ada[l], 'b_ada': b_ada[l], 'norm1_g': norm1_g[l], 'w_in': w_in[l],
             'cmp_pe_k': cmp_pe_k[l], 'cmp_w1_k': cmp_w1_k[l], 'cmp_w2_k': cmp_w2_k[l],
             'cmp_pe_v': cmp_pe_v[l], 'cmp_w1_v': cmp_w1_v[l], 'cmp_w2_v': cmp_w2_v[l],
             'out_g_sb': out_g_sb[l], 'out_g_nsa': out_g_nsa[l], 'w_out': w_out[l],
             'norm2_g': norm2_g[l], 'peer_wq': peer_wq[l], 'peer_k1': peer_k1[l],
             'peer_k2': peer_k2[l], 'peer_u': peer_u[l], 'peer_v': peer_v[l]}
        yp, sp = trunk_block(yp, c_prompt, lambda h: prompt_mix(h, W), W)
        ys, ss = trunk_block(ys, c_sample, lambda h: sample_mix(
            h, cache_sb[l], cache_cmp[l], cache_slc[l], state_win[l], page_table, W), W)
        st_p.append(sp)
        st_s.append(ss)
    stack = lambda sts, i: jnp.stack([st[i] for st in sts])
    y_prompt = rmsnorm(yp, final_g)
    y_sample = rmsnorm(ys, final_g)
    return (y_prompt, y_sample, stack(st_p, 0), stack(st_s, 0), stack(st_p, 1), stack(st_s, 1),
            stack(st_p, 2), stack(st_s, 2), stack(st_p, 3), stack(st_s, 3))
```

```python
import functools

import jax
import jax.numpy as jnp
from jax import lax
from jax.experimental import pallas as pl
from jax.experimental.pallas import tpu as pltpu

F32 = jnp.float32
BF16 = jnp.bfloat16

HEAD_DIM = 128
H_SB = 8
H_NSA = 8
G_NSA = 2
HPG = H_NSA // G_NSA
SB_W = H_SB * HEAD_DIM
NSA_W = H_NSA * HEAD_DIM
KV_W = 2 * G_NSA * HEAD_DIM
D_IN = 3 * SB_W + NSA_W + 3 * KV_W + 3 * H_NSA
SCALE = HEAD_DIM ** -0.5
ROT_HALF = HEAD_DIM // 8
ROPE_THETA = 500000.0
CMP_BLK = 64
TOP_BLOCKS = 16
WINDOW = 512
FORCE_SCORE = 1e9
N_KEYS = 128
PEER_HEADS = 8
PEER_TOPK = 16
D_QUERY = 256
EPS = 1e-6
NEG_INF = -1e30

LANES = 128
SUBLANES = 8
VMEM_LIMIT = 56 * 1024 * 1024

IN_TN = 256
IN_NT = -(-D_IN // IN_TN)
PEER_TN = 640
PEER_AT = 4
PAGES_PER_STEP_SB = 8
PAGES_PER_STEP_NSA = 16


def _cparams(sem):
    return pltpu.CompilerParams(dimension_semantics=sem, vmem_limit_bytes=VMEM_LIMIT)


def _dot(a, b):
    return jnp.dot(a, b, preferred_element_type=F32)


def _dot_nt(a, b):
    return lax.dot_general(a, b, (((1,), (1,)), ((), ())), preferred_element_type=F32)


def _dot_tn(a, b):
    return lax.dot_general(a, b, (((0,), (0,)), ((), ())), preferred_element_type=F32)


def _gelu(x):
    return 0.5 * x * (1.0 + jnp.tanh(0.7978845608028654 * (x + 0.044715 * (x * x * x))))


def _rms(x, g):
    return x * lax.rsqrt(jnp.mean(x * x, axis=-1, keepdims=True) + EPS) * g


def _softplus(z):
    return jnp.maximum(z, 0.0) + jnp.log(1.0 + jnp.exp(-jnp.abs(z)))


def _split_bf16(x):
    hi = x.astype(BF16)
    lo = (x - hi.astype(F32)).astype(BF16)
    return hi, lo


def _mod_kernel(c_ref, w_ref, b_ref, o_ref):
    o_ref[...] = _dot(c_ref[...], w_ref[...].astype(BF16)) + b_ref[...]


def adaln_mod(c_bf, w_ada, b_ada):
    m, d = c_bf.shape
    n = w_ada.shape[1]
    tn = 1024
    return pl.pallas_call(
        _mod_kernel,
        out_shape=jax.ShapeDtypeStruct((m, n), F32),
        grid=(n // tn,),
        in_specs=[pl.BlockSpec((m, d), lambda j: (0, 0)),
                  pl.BlockSpec((d, tn), lambda j: (0, j)),
                  pl.BlockSpec((1, tn), lambda j: (0, j))],
        out_specs=pl.BlockSpec((m, tn), lambda j: (0, j)),
        compiler_params=_cparams(("arbitrary",)),
        name="adaln_mod",
    )(c_bf, w_ada, b_ada)


_IN_SEGMENTS = (
    (0, 4, 0, "plain"),
    (4, 12, 1, "plain"),
    (12, 16, 2, "rope"),
    (16, 17, 3, "rope"),
    (17, 18, 3, "plain"),
    (18, 19, 4, "rope"),
    (19, 20, 4, "plain"),
    (20, 21, 5, "rope"),
    (21, 22, 5, "plain"),
    (22, 23, 6, "gate"),
)
_IN_OUT_RANGES = ((0, 4), (4, 12), (12, 16), (16, 18), (18, 20), (20, 22), (22, 23))


def _inproj_kernel(x_ref, g_ref, shift_ref, scale_ref, w_ref, rc_ref, ra_ref, rb_ref,
                   *rest):
    outs, h_scr = rest[:7], rest[7]
    j = pl.program_id(1)

    @pl.when(j == 0)
    def _():
        h = _rms(x_ref[...], g_ref[...]) * (1.0 + scale_ref[0]) + shift_ref[0]
        h_scr[...] = h.astype(BF16)

    p = _dot(h_scr[...], w_ref[...])
    for lo, hi, slot, kind in _IN_SEGMENTS:
        @pl.when((j >= lo) & (j < hi))
        def _(slot=slot, kind=kind):
            if kind == "rope":
                v = (p * rc_ref[...] + pltpu.roll(p, ROT_HALF, 1) * ra_ref[...]
                     + pltpu.roll(p, IN_TN - ROT_HALF, 1) * rb_ref[...])
            elif kind == "gate":
                v = jax.nn.sigmoid(p)
            else:
                v = p
            outs[slot][...] = v.astype(outs[slot].dtype)


def _rope_tables(pos):
    inv = jnp.power(ROPE_THETA, -jnp.arange(ROT_HALF, dtype=F32) / ROT_HALF)
    ang = pos.astype(F32)[:, None] * inv[None, :]
    cos, sin = jnp.cos(ang), jnp.sin(ang)
    n = pos.shape[0]
    z16 = jnp.zeros((n, ROT_HALF), F32)
    tail0 = jnp.zeros((n, HEAD_DIM - 2 * ROT_HALF), F32)
    rc = jnp.concatenate([cos, cos, tail0 + 1.0], axis=1)
    ra = jnp.concatenate([z16, sin, tail0], axis=1)
    rb = jnp.concatenate([-sin, z16, tail0], axis=1)
    heads = IN_TN // HEAD_DIM
    return tuple(jnp.tile(t, (1, heads)) for t in (rc, ra, rb))


def in_projection(x2d, g1, shift, scale, w_bf, tables, tm, mod_map, tab_map):
    r, d = x2d.shape
    widths = [(hi - lo) * IN_TN for lo, hi in _IN_OUT_RANGES]
    dtypes = [BF16, F32, BF16, F32, F32, F32, F32]
    out_shape = [jax.ShapeDtypeStruct((r, w), dt) for w, dt in zip(widths, dtypes)]

    def out_map(lo, hi):
        return lambda i, j: (i, jnp.clip(j - lo, 0, hi - lo - 1))

    mod_block = (1,) + shift.shape[1:]
    return pl.pallas_call(
        _inproj_kernel,
        out_shape=out_shape,
        grid=(r // tm, IN_NT),
        in_specs=[pl.BlockSpec((tm, d), lambda i, j: (i, 0)),
                  pl.BlockSpec((1, d), lambda i, j: (0, 0)),
                  pl.BlockSpec(mod_block, lambda i, j: (mod_map(i), 0, 0)),
                  pl.BlockSpec(mod_block, lambda i, j: (mod_map(i), 0, 0)),
                  pl.BlockSpec((d, IN_TN), lambda i, j: (0, j)),
                  pl.BlockSpec((tm, IN_TN), lambda i, j: (tab_map(i), 0)),
                  pl.BlockSpec((tm, IN_TN), lambda i, j: (tab_map(i), 0)),
                  pl.BlockSpec((tm, IN_TN), lambda i, j: (tab_map(i), 0))],
        out_specs=[pl.BlockSpec((tm, IN_TN), out_map(lo, hi)) for lo, hi in _IN_OUT_RANGES],
        scratch_shapes=[pltpu.VMEM((tm, d), BF16)],
        compiler_params=_cparams(("arbitrary", "arbitrary")),
        name="in_projection",
    )(x2d, g1, shift, scale, w_bf, *tables)


def _sb_prompt_kernel(q_ref, k_ref, v_ref, o_ref, *, tq, tk):
    qi = pl.program_id(2)
    q = q_ref[...]
    qpos = qi * tq + lax.broadcasted_iota(jnp.int32, (tq, tk), 0)
    lane = lax.broadcasted_iota(jnp.int32, (tq, tk), 1)
    tri = (lax.broadcasted_iota(jnp.int32, (tk, tk), 0)
           > lax.broadcasted_iota(jnp.int32, (tk, tk), 1)).astype(BF16)
    nk = (qi + 1) * (tq // tk)

    def body(it, carry):
        later, acc = carry
        start = pl.multiple_of((nk - 1 - it) * tk, tk)
        k = k_ref[pl.ds(start, tk), :].astype(BF16)
        v = v_ref[pl.ds(start, tk), :].astype(BF16)
        z = _dot_nt(q, k) * SCALE
        valid = (start + lane) < qpos
        sp = _softplus(z)
        log1m = jnp.where(valid, -sp, 0.0)
        hi, lo = _split_bf16(log1m)
        after = _dot(hi, tri) + _dot(lo, tri) + later
        a = jnp.where(valid, jnp.exp(z - sp + after), 0.0)
        acc = acc + _dot(a.astype(BF16), v)
        later = later + jnp.sum(log1m, axis=1, keepdims=True)
        return later, acc

    _, acc = lax.fori_loop(0, nk, body, (jnp.zeros((tq, 1), F32), jnp.zeros((tq, HEAD_DIM), F32)))
    o_ref[...] = acc


def sb_prompt(q_sb, kv_sb, b, t):
    tq, tk = 256, 128
    nq = t // tq
    return pl.pallas_call(
        functools.partial(_sb_prompt_kernel, tq=tq, tk=tk),
        out_shape=jax.ShapeDtypeStruct((b * t, SB_W), F32),
        grid=(b, H_SB, nq),
        in_specs=[pl.BlockSpec((tq, HEAD_DIM), lambda bi, h, qi: (bi * nq + qi, h)),
                  pl.BlockSpec((t, HEAD_DIM), lambda bi, h, qi: (bi, h)),
                  pl.BlockSpec((t, HEAD_DIM), lambda bi, h, qi: (bi, H_SB + h))],
        out_specs=pl.BlockSpec((tq, HEAD_DIM), lambda bi, h, qi: (bi * nq + qi, h)),
        compiler_params=_cparams(("arbitrary", "arbitrary", "arbitrary")),
        name="sb_prompt",
    )(q_sb, kv_sb, kv_sb)


def _compress_rows(read_rows, pe_ref, w1_ref, w2_ref, c, nrows):
    def body(p, acc):
        x = jnp.concatenate([read_rows(p, 0), read_rows(p, 1)], axis=0) + pe_ref[c, pl.ds(p, 1), :]
        return acc + _dot(x.astype(BF16), w1_ref[c, p])

    acc = lax.fori_loop(0, CMP_BLK, body, jnp.zeros((2 * nrows, HEAD_DIM), F32))
    return _dot(_gelu(acc).astype(BF16), w2_ref[c])


def _cmp_prompt_kernel(x_ref, pe_ref, w1_ref, w2_ref, o_ref, *, nb):
    stride = CMP_BLK * 2 * G_NSA
    for c in range(2):
        def read_rows(p, g, c=c):
            return x_ref[0, pl.ds(p * 2 * G_NSA + c * G_NSA + g, nb, stride=stride), :]
        out = _compress_rows(read_rows, pe_ref, w1_ref, w2_ref, c, nb)
        o_ref[0, c, 0] = out[:nb]
        o_ref[0, c, 1] = out[nb:]


def compress_prompt(kv_cmp, pe, w1, w2, b, t):
    nb = t // CMP_BLK
    x = kv_cmp.reshape(b, t * 2 * G_NSA, HEAD_DIM)
    return pl.pallas_call(
        functools.partial(_cmp_prompt_kernel, nb=nb),
        out_shape=jax.ShapeDtypeStruct((b, 2, G_NSA, nb, HEAD_DIM), F32),
        grid=(b,),
        in_specs=[pl.BlockSpec((1, t * 2 * G_NSA, HEAD_DIM), lambda i: (i, 0, 0)),
                  pl.BlockSpec(pe.shape, lambda i: (0, 0, 0)),
                  pl.BlockSpec(w1.shape, lambda i: (0, 0, 0, 0)),
                  pl.BlockSpec(w2.shape, lambda i: (0, 0, 0))],
        out_specs=pl.BlockSpec((1, 2, G_NSA, nb, HEAD_DIM), lambda i: (i, 0, 0, 0, 0)),
        compiler_params=_cparams(("arbitrary",)),
        name="compress_prompt",
    )(x, pe, w1, w2)


def _cmp_sample_kernel(pt_ref, *refs, pages, nbp, past):
    page_refs = refs[:pages]
    new_ref, pe_ref, w1_ref, w2_ref, o_ref, buf = refs[pages:]
    b, jb = pl.program_id(0), pl.program_id(1)
    rows_per_page = page_refs[0].shape[1]
    past_rows = past * 2 * G_NSA

    @pl.when((b == 0) & (jb == 0))
    def _():
        tail = buf.shape[0] - past_rows
        buf[pl.ds(past_rows, tail), :] = jnp.zeros((tail, HEAD_DIM), F32)

    for i, pr in enumerate(page_refs):
        start = pl.multiple_of((jb * pages + i) * rows_per_page, rows_per_page)
        buf[pl.ds(start, rows_per_page), :] = pr[0]

    @pl.when(jb == pl.num_programs(1) - 1)
    def _():
        buf[pl.ds(past_rows, new_ref.shape[1]), :] = new_ref[0]
        stride = CMP_BLK * 2 * G_NSA
        for c in range(2):
            def read_rows(p, g, c=c):
                return buf[pl.ds(p * 2 * G_NSA + c * G_NSA + g, nbp, stride=stride), :]
            out = _compress_rows(read_rows, pe_ref, w1_ref, w2_ref, c, nbp)
            o_ref[0, c, 0] = out[:nbp]
            o_ref[0, c, 1] = out[nbp:]


def compress_sample(cache_cmp, page_table, kv_new, pe, w1, w2):
    n_pool, page = cache_cmp.shape[:2]
    bsz, n_pages = page_table.shape
    nq = kv_new.shape[1]
    past = n_pages * page
    nb = -(-(past + nq) // CMP_BLK)
    nbp = -(-nb // SUBLANES) * SUBLANES
    pages = min(PAGES_PER_STEP_NSA, n_pages)
    rpp = page * 2 * G_NSA
    cache = cache_cmp.reshape(n_pool, rpp, HEAD_DIM)
    new = kv_new.reshape(bsz, nq * 2 * G_NSA, HEAD_DIM)

    def page_map(i):
        return lambda b, jb, pt: (pt[b, jb * pages + i], 0, 0)

    grid_spec = pltpu.PrefetchScalarGridSpec(
        num_scalar_prefetch=1,
        grid=(bsz, n_pages // pages),
        in_specs=[pl.BlockSpec((1, rpp, HEAD_DIM), page_map(i)) for i in range(pages)]
        + [pl.BlockSpec((1,) + new.shape[1:], lambda b, jb, pt: (b, 0, 0)),
           pl.BlockSpec(pe.shape, lambda b, jb, pt: (0, 0, 0)),
           pl.BlockSpec(w1.shape, lambda b, jb, pt: (0, 0, 0, 0)),
           pl.BlockSpec(w2.shape, lambda b, jb, pt: (0, 0, 0))],
        out_specs=pl.BlockSpec((1, 2, G_NSA, nbp, HEAD_DIM), lambda b, jb, pt: (b, 0, 0, 0, 0)),
        scratch_shapes=[pltpu.VMEM((nbp * CMP_BLK * 2 * G_NSA, HEAD_DIM), F32)],
    )
    return pl.pallas_call(
        functools.partial(_cmp_sample_kernel, pages=pages, nbp=nbp, past=past),
        out_shape=jax.ShapeDtypeStruct((bsz, 2, G_NSA, nbp, HEAD_DIM), F32),
        grid_spec=grid_spec,
        compiler_params=_cparams(("arbitrary", "arbitrary")),
        name="compress_sample",
    )(page_table, *([cache] * pages), new, pe, w1, w2)


def _cmp_branch(q4, qpos4, kc, vc, nq, nbp):
    blk = lax.broadcasted_iota(jnp.int32, (1, nbp), 1)
    s = _dot_nt(q4, kc.astype(BF16)) * SCALE
    complete = (blk + 1) * CMP_BLK <= qpos4 + 1
    m = jnp.max(jnp.where(complete, s, NEG_INF), axis=1, keepdims=True)
    e = jnp.where(complete, jnp.exp(s - m), 0.0)
    den = jnp.sum(e, axis=1, keepdims=True)
    p = e / jnp.where(den > 0.0, den, 1.0)
    o_cmp = _dot(p.astype(BF16), vc.astype(BF16))
    imp = p[0:nq]
    for n in range(1, HPG):
        imp = imp + p[n * nq:(n + 1) * nq]
    qpos = qpos4[0:nq]
    cur = qpos // CMP_BLK
    started = blk <= cur
    forced = started & ((blk == 0) | (blk >= cur - 1))
    score = jnp.where(forced, FORCE_SCORE, jnp.where(started, imp, -FORCE_SCORE))
    rank = jnp.zeros((nq, nbp), F32)
    for j in range(nbp):
        col = score[:, j:j + 1]
        rank = rank + jnp.where(col > score, 1.0, 0.0) + jnp.where((col == score) & (blk > j), 1.0, 0.0)
    return o_cmp, rank


def _flash_tile(q4, k, v, mask, state):
    m, l, acc = state
    s = _dot_nt(q4, k) * SCALE
    m_new = jnp.maximum(m, jnp.max(jnp.where(mask, s, NEG_INF), axis=1, keepdims=True))
    p = jnp.where(mask, jnp.exp(s - m_new), 0.0)
    alpha = jnp.exp(m - m_new)
    l = alpha * l + jnp.sum(p, axis=1, keepdims=True)
    acc = alpha * acc + _dot(p.astype(BF16), v)
    return m_new, l, acc


def _flash_init(rows):
    return (jnp.full((rows, 1), NEG_INF, F32), jnp.zeros((rows, 1), F32),
            jnp.zeros((rows, HEAD_DIM), F32))


def _flash_out(state):
    _, l, acc = state
    return acc / jnp.where(l > 0.0, l, 1.0)


def _stack_heads(q_ref):
    return jnp.concatenate([q_ref[:, n * HEAD_DIM:(n + 1) * HEAD_DIM] for n in range(HPG)], axis=0)


def _nsa_prompt_kernel(q_ref, kc_ref, vc_ref, sk_ref, sv_ref, wk_ref, wv_ref, gate_ref, o_ref,
                       selx_scr, *, tq, tk, nb, topk):
    g, qi = pl.program_id(1), pl.program_id(2)
    rows = HPG * tq
    q4 = _stack_heads(q_ref)
    t_in = lax.broadcasted_iota(jnp.int32, (tq, 1), 0)
    qpos1 = qi * tq + t_in
    qpos4 = jnp.concatenate([qpos1] * HPG, axis=0)
    o_cmp, rank = _cmp_branch(q4, qpos4, kc_ref[0, 0, 0], vc_ref[0, 0, 0], tq, nb)
    sel = jnp.where(rank < topk, 1.0, 0.0).astype(BF16)
    nkt = selx_scr.shape[0]
    kb = lax.broadcasted_iota(jnp.int32, (nb, tk), 0)
    kl = lax.broadcasted_iota(jnp.int32, (nb, tk), 1)
    for j in range(nkt):
        expand = jnp.where((j * tk + kl) // CMP_BLK == kb, 1.0, 0.0).astype(BF16)
        selx_scr[j] = _dot(sel, expand)
    lane = lax.broadcasted_iota(jnp.int32, (rows, tk), 1)

    def slc_body(j, state):
        start = pl.multiple_of(j * tk, tk)
        k = sk_ref[pl.ds(start, tk), :].astype(BF16)
        v = sv_ref[pl.ds(start, tk), :].astype(BF16)
        sx = selx_scr[j]
        mask = (jnp.concatenate([sx] * HPG, axis=0) > 0.5) & ((start + lane) <= qpos4)
        return _flash_tile(q4, k, v, mask, state)

    o_slc = _flash_out(lax.fori_loop(0, (qi * tq) // tk + tq // tk, slc_body, _flash_init(rows)))

    def win_body(j, state):
        start = pl.multiple_of(j * tk, tk)
        k = wk_ref[pl.ds(start, tk), :].astype(BF16)
        v = wv_ref[pl.ds(start, tk), :].astype(BF16)
        d = qpos4 - (start + lane)
        mask = (d >= 0) & (d < WINDOW)
        return _flash_tile(q4, k, v, mask, state)

    first = jnp.maximum(qi * tq - WINDOW, 0) // tk
    o_win = _flash_out(lax.fori_loop(first, (qi * tq) // tk + tq // tk, win_body, _flash_init(rows)))

    gates = gate_ref[...]

    def gate(n, branch):
        col = gates[:, n * 3 + branch:n * 3 + branch + 1]
        for gg in range(1, G_NSA):
            c = (gg * HPG + n) * 3 + branch
            col = jnp.where(g == gg, gates[:, c:c + 1], col)
        return col

    for n in range(HPG):
        r0 = n * tq
        o_ref[:, n * HEAD_DIM:(n + 1) * HEAD_DIM] = (
            gate(n, 0) * o_cmp[r0:r0 + tq] + gate(n, 1) * o_slc[r0:r0 + tq]
            + gate(n, 2) * o_win[r0:r0 + tq])


def nsa_prompt(q_nsa, kvc, kv_slc, kv_win, gates, b, t):
    tq = tk = 128
    nq = t // tq
    nb = t // CMP_BLK
    topk = min(TOP_BLOCKS, nb)
    gw = HPG * HEAD_DIM
    return pl.pallas_call(
        functools.partial(_nsa_prompt_kernel, tq=tq, tk=tk, nb=nb, topk=topk),
        out_shape=jax.ShapeDtypeStruct((b * t, NSA_W), F32),
        grid=(b, G_NSA, nq),
        in_specs=[pl.BlockSpec((tq, gw), lambda bi, g, qi: (bi * nq + qi, g)),
                  pl.BlockSpec((1, 1, 1, nb, HEAD_DIM), lambda bi, g, qi: (bi, 0, g, 0, 0)),
                  pl.BlockSpec((1, 1, 1, nb, HEAD_DIM), lambda bi, g, qi: (bi, 1, g, 0, 0)),
                  pl.BlockSpec((t, HEAD_DIM), lambda bi, g, qi: (bi, g)),
                  pl.BlockSpec((t, HEAD_DIM), lambda bi, g, qi: (bi, G_NSA + g)),
                  pl.BlockSpec((t, HEAD_DIM), lambda bi, g, qi: (bi, g)),
                  pl.BlockSpec((t, HEAD_DIM), lambda bi, g, qi: (bi, G_NSA + g)),
                  pl.BlockSpec((tq, IN_TN), lambda bi, g, qi: (bi * nq + qi, 0))],
        out_specs=pl.BlockSpec((tq, gw), lambda bi, g, qi: (bi * nq + qi, g)),
        scratch_shapes=[pltpu.VMEM((t // tk, tq, tk), F32)],
        compiler_params=_cparams(("arbitrary", "arbitrary", "arbitrary")),
        name="nsa_prompt",
    )(q_nsa, kvc, kvc, kv_slc, kv_slc, kv_win, kv_win, gates)


def _sb_sample_kernel(pt_ref, *refs, pages, nq):
    page_refs = refs[:pages]
    qbd_ref, new_ref, o_ref, acc_scr, later_scr = refs[pages:]
    jb = pl.program_id(1)
    page = page_refs[0].shape[1]
    cols = H_SB * nq
    tri = (lax.broadcasted_iota(jnp.int32, (page, page), 1)
           > lax.broadcasted_iota(jnp.int32, (page, page), 0)).astype(BF16)
    qbd = qbd_ref[0]

    def sweep(kv, valid):
        z = _dot(kv[:, :SB_W].astype(BF16), qbd) * SCALE
        sp = _softplus(z)
        log1m = -sp if valid is None else jnp.where(valid, -sp, 0.0)
        hi, lo = _split_bf16(log1m)
        after = _dot(tri, hi) + _dot(tri, lo) + later_scr[...]
        a = jnp.exp(z - sp + after)
        if valid is not None:
            a = jnp.where(valid, a, 0.0)
        acc_scr[...] += _dot_tn(a.astype(BF16), kv[:, SB_W:].astype(BF16))
        later_scr[...] += jnp.sum(log1m, axis=0, keepdims=True)

    @pl.when(jb == 0)
    def _():
        acc_scr[...] = jnp.zeros_like(acc_scr)
        later_scr[...] = jnp.zeros_like(later_scr)
        key = lax.broadcasted_iota(jnp.int32, (page, cols), 0)
        qry = lax.broadcasted_iota(jnp.int32, (page, cols), 1) % nq
        sweep(new_ref[0], key < qry)

    for pr in reversed(page_refs):
        sweep(pr[0], None)

    @pl.when(jb == pl.num_programs(1) - 1)
    def _():
        acc = acc_scr[...]
        for h in range(H_SB):
            o_ref[0, :, h * HEAD_DIM:(h + 1) * HEAD_DIM] = acc[h * nq:(h + 1) * nq,
                                                               h * HEAD_DIM:(h + 1) * HEAD_DIM]


def sb_sample(cache_sb, page_table, q_sb, kv_new):
    n_pool, page = cache_sb.shape[:2]
    bsz, n_pages = page_table.shape
    nq = q_sb.shape[1]
    pages = min(PAGES_PER_STEP_SB, n_pages)
    nsteps = n_pages // pages
    cache = cache_sb.reshape(n_pool, page, 2 * SB_W)
    qh = q_sb.reshape(bsz, nq, H_SB, HEAD_DIM).transpose(0, 2, 3, 1)
    qbd = (qh[:, :, :, None, :] * jnp.eye(H_SB, dtype=BF16)[None, :, None, :, None])
    qbd = qbd.reshape(bsz, SB_W, H_SB * nq)
    new_page = jnp.pad(kv_new, ((0, 0), (0, page - nq), (0, 0)))

    def page_map(i):
        return lambda b, jb, pt: (pt[b, n_pages - 1 - (jb * pages + (pages - 1 - i))], 0, 0)

    grid_spec = pltpu.PrefetchScalarGridSpec(
        num_scalar_prefetch=1,
        grid=(bsz, nsteps),
        in_specs=[pl.BlockSpec((1, page, 2 * SB_W), page_map(i)) for i in range(pages)]
        + [pl.BlockSpec((1, SB_W, H_SB * nq), lambda b, jb, pt: (b, 0, 0)),
           pl.BlockSpec((1, page, 2 * SB_W), lambda b, jb, pt: (b, 0, 0))],
        out_specs=pl.BlockSpec((1, nq, SB_W), lambda b, jb, pt: (b, 0, 0)),
        scratch_shapes=[pltpu.VMEM((H_SB * nq, SB_W), F32), pltpu.VMEM((1, H_SB * nq), F32)],
    )
    return pl.pallas_call(
        functools.partial(_sb_sample_kernel, pages=pages, nq=nq),
        out_shape=jax.ShapeDtypeStruct((bsz, nq, SB_W), F32),
        grid_spec=grid_spec,
        compiler_params=_cparams(("arbitrary", "arbitrary")),
        name="sb_sample",
    )(page_table, *([cache] * pages), qbd, new_page)


def _nsa_sample_sel_kernel(q_ref, kc_ref, vc_ref, st_ref, new_ref, ocmp_ref, owin_ref, selx_ref,
                           *, nq, nbp, past, topk):
    rows = HPG * nq
    wb = st_ref.shape[1]
    t_in = lax.broadcasted_iota(jnp.int32, (nq, 1), 0)
    qpos4 = jnp.concatenate([past + t_in] * HPG, axis=0)
    kb = lax.broadcasted_iota(jnp.int32, (nbp, nbp * CMP_BLK), 0)
    kl = lax.broadcasted_iota(jnp.int32, (nbp, nbp * CMP_BLK), 1)
    expand = jnp.where(kl // CMP_BLK == kb, 1.0, 0.0).astype(BF16)
    lane_w = lax.broadcasted_iota(jnp.int32, (rows, wb), 1)
    lane_n = lax.broadcasted_iota(jnp.int32, (rows, new_ref.shape[1]), 1)
    for g in range(G_NSA):
        q4 = jnp.concatenate([q_ref[0, :, (g * HPG + n) * HEAD_DIM:(g * HPG + n + 1) * HEAD_DIM]
                              for n in range(HPG)], axis=0)
        o_cmp, rank = _cmp_branch(q4, qpos4, kc_ref[0, 0, g], vc_ref[0, 0, g], nq, nbp)
        sel = jnp.where(rank < topk, 1.0, 0.0).astype(BF16)
        selx_ref[0, g] = _dot(sel, expand)
        kcol, vcol = g * HEAD_DIM, (G_NSA + g) * HEAD_DIM
        state = _flash_init(rows)
        d = qpos4 - (past - wb + lane_w)
        state = _flash_tile(q4, st_ref[0, :, kcol:kcol + HEAD_DIM].astype(BF16),
                            st_ref[0, :, vcol:vcol + HEAD_DIM].astype(BF16),
                            (d >= 0) & (d < WINDOW), state)
        d = qpos4 - (past + lane_n)
        state = _flash_tile(q4, new_ref[0, :, kcol:kcol + HEAD_DIM].astype(BF16),
                            new_ref[0, :, vcol:vcol + HEAD_DIM].astype(BF16),
                            (d >= 0) & (d < WINDOW) & (lane_n < nq), state)
        o_win = _flash_out(state)
        for n in range(HPG):
            h = g * HPG + n
            ocmp_ref[0, :, h * HEAD_DIM:(h + 1) * HEAD_DIM] = o_cmp[n * nq:(n + 1) * nq]
            owin_ref[0, :, h * HEAD_DIM:(h + 1) * HEAD_DIM] = o_win[n * nq:(n + 1) * nq]


def nsa_sample_select(q_nsa, kvc, state_win, win_new, past):
    bsz, nq = q_nsa.shape[:2]
    nbp = kvc.shape[3]
    nb = -(-(past + nq) // CMP_BLK)
    wb = state_win.shape[1]
    return pl.pallas_call(
        functools.partial(_nsa_sample_sel_kernel, nq=nq, nbp=nbp, past=past, topk=min(TOP_BLOCKS, nb)),
        out_shape=[jax.ShapeDtypeStruct((bsz, nq, NSA_W), F32),
                   jax.ShapeDtypeStruct((bsz, nq, NSA_W), F32),
                   jax.ShapeDtypeStruct((bsz, G_NSA, nq, nbp * CMP_BLK), F32)],
        grid=(bsz,),
        in_specs=[pl.BlockSpec((1, nq, NSA_W), lambda b: (b, 0, 0)),
                  pl.BlockSpec((1, 1, G_NSA, nbp, HEAD_DIM), lambda b: (b, 0, 0, 0, 0)),
                  pl.BlockSpec((1, 1, G_NSA, nbp, HEAD_DIM), lambda b: (b, 1, 0, 0, 0)),
                  pl.BlockSpec((1, wb, KV_W), lambda b: (b, 0, 0)),
                  pl.BlockSpec((1,) + win_new.shape[1:], lambda b: (b, 0, 0))],
        out_specs=[pl.BlockSpec((1, nq, NSA_W), lambda b: (b, 0, 0)),
                   pl.BlockSpec((1, nq, NSA_W), lambda b: (b, 0, 0)),
                   pl.BlockSpec((1, G_NSA, nq, nbp * CMP_BLK), lambda b: (b, 0, 0, 0))],
        compiler_params=_cparams(("arbitrary",)),
        name="nsa_sample_select",
    )(q_nsa, kvc, kvc, state_win, win_new)


def _nsa_sample_slc_kernel(pt_ref, *refs, pages, nq, past):
    page_refs = refs[:pages]
    (q_ref, new_ref, selx_ref, selx_new_ref, ocmp_ref, owin_ref, gate_ref, o_ref,
     m_scr, l_scr, acc_scr) = refs[pages:]
    jb = pl.program_id(1)
    rows = HPG * nq
    page = page_refs[0].shape[1]
    t_in = lax.broadcasted_iota(jnp.int32, (nq, 1), 0)
    qpos4 = jnp.concatenate([past + t_in] * HPG, axis=0)
    q4s = [jnp.concatenate([q_ref[0, :, (g * HPG + n) * HEAD_DIM:(g * HPG + n + 1) * HEAD_DIM]
                            for n in range(HPG)], axis=0) for g in range(G_NSA)]

    @pl.when(jb == 0)
    def _():
        m_scr[...] = jnp.full(m_scr.shape, NEG_INF, F32)
        l_scr[...] = jnp.zeros_like(l_scr)
        acc_scr[...] = jnp.zeros_like(acc_scr)

    def sweep(kv, sx_of_g, causal):
        for g in range(G_NSA):
            kcol, vcol = g * HEAD_DIM, (G_NSA + g) * HEAD_DIM
            sx = sx_of_g(g)
            mask = jnp.concatenate([sx] * HPG, axis=0) > 0.5
            if causal is not None:
                mask = mask & causal
            state = (m_scr[g], l_scr[g], acc_scr[g])
            m, l, acc = _flash_tile(q4s[g], kv[:, kcol:kcol + HEAD_DIM].astype(BF16),
                                    kv[:, vcol:vcol + HEAD_DIM].astype(BF16), mask, state)
            m_scr[g], l_scr[g], acc_scr[g] = m, l, acc

    for i, pr in enumerate(page_refs):
        sweep(pr[0], lambda g, i=i: selx_ref[0, g, :, i * page:(i + 1) * page], None)

    @pl.when(jb == pl.num_programs(1) - 1)
    def _():
        nrow = new_ref.shape[1]
        lane_n = lax.broadcasted_iota(jnp.int32, (rows, nrow), 1)
        sweep(new_ref[0], lambda g: selx_new_ref[0, g, :, 0:nrow],
              ((past + lane_n) <= qpos4) & (lane_n < nq))
        gates = gate_ref[0]
        for g in range(G_NSA):
            o_slc = _flash_out((m_scr[g], l_scr[g], acc_scr[g]))
            for n in range(HPG):
                h = g * HPG + n
                cols = slice(h * HEAD_DIM, (h + 1) * HEAD_DIM)
                o_ref[0, :, cols] = (gates[:, 3 * h:3 * h + 1] * ocmp_ref[0, :, cols]
                                     + gates[:, 3 * h + 1:3 * h + 2] * o_slc[n * nq:(n + 1) * nq]
                                     + gates[:, 3 * h + 2:3 * h + 3] * owin_ref[0, :, cols])


def nsa_sample_slc(cache_slc, page_table, q_nsa, slc_new, selx, o_cmp, o_win, gates):
    n_pool, page = cache_slc.shape[:2]
    bsz, n_pages = page_table.shape
    nq = q_nsa.shape[1]
    past = n_pages * page
    pages = min(PAGES_PER_STEP_NSA, n_pages)
    cache = cache_slc.reshape(n_pool, page, KV_W)
    rows = HPG * nq

    def page_map(i):
        return lambda b, jb, pt: (pt[b, jb * pages + i], 0, 0)

    per_b = lambda b, jb, pt: (b, 0, 0)
    grid_spec = pltpu.PrefetchScalarGridSpec(
        num_scalar_prefetch=1,
        grid=(bsz, n_pages // pages),
        in_specs=[pl.BlockSpec((1, page, KV_W), page_map(i)) for i in range(pages)]
        + [pl.BlockSpec((1, nq, NSA_W), per_b),
           pl.BlockSpec((1,) + slc_new.shape[1:], per_b),
           pl.BlockSpec((1, G_NSA, nq, pages * page), lambda b, jb, pt: (b, 0, 0, jb)),
           pl.BlockSpec((1, G_NSA, nq, LANES), lambda b, jb, pt: (b, 0, 0, past // LANES)),
           pl.BlockSpec((1, nq, NSA_W), per_b),
           pl.BlockSpec((1, nq, NSA_W), per_b),
           pl.BlockSpec((1, nq, IN_TN), per_b)],
        out_specs=pl.BlockSpec((1, nq, NSA_W), per_b),
        scratch_shapes=[pltpu.VMEM((G_NSA, rows, 1), F32), pltpu.VMEM((G_NSA, rows, 1), F32),
                        pltpu.VMEM((G_NSA, rows, HEAD_DIM), F32)],
    )
    return pl.pallas_call(
        functools.partial(_nsa_sample_slc_kernel, pages=pages, nq=nq, past=past),
        out_shape=jax.ShapeDtypeStruct((bsz, nq, NSA_W), F32),
        grid_spec=grid_spec,
        compiler_params=_cparams(("arbitrary", "arbitrary")),
        name="nsa_sample_slc",
    )(page_table, *([cache] * pages), q_nsa, slc_new, selx, selx, o_cmp, o_win, gates)


def _outproj_kernel(osb_ref, onsa_ref, gsb_ref, gnsa_ref, wa_ref, wb_ref, x_ref, gate_ref,
                    g2_ref, shift_ref, scale_ref, x1_ref, h2_ref):
    a = _rms(osb_ref[...], gsb_ref[...]).astype(BF16)
    b = _rms(onsa_ref[...], gnsa_ref[...]).astype(BF16)
    mixed = _dot(a, wa_ref[...]) + _dot(b, wb_ref[...])
    x1 = x_ref[...] + gate_ref[0] * mixed
    x1_ref[...] = x1
    h2_ref[...] = (_rms(x1, g2_ref[...]) * (1.0 + scale_ref[0]) + shift_ref[0]).astype(BF16)


def out_projection(o_sb, o_nsa, g_sb, g_nsa, w_out_bf, x2d, gate, g2, shift, scale, tm, mod_map):
    r, d = x2d.shape
    mod_block = (1,) + gate.shape[1:]
    mod_spec = pl.BlockSpec(mod_block, lambda i: (mod_map(i), 0, 0))
    row = lambda w: pl.BlockSpec((tm, w), lambda i: (i, 0))
    const = lambda shp: pl.BlockSpec(shp, lambda i: (0,) * len(shp))
    return pl.pallas_call(
        _outproj_kernel,
        out_shape=[jax.ShapeDtypeStruct((r, d), F32), jax.ShapeDtypeStruct((r, d), BF16)],
        grid=(r // tm,),
        in_specs=[row(SB_W), row(NSA_W), const((1, SB_W)), const((1, NSA_W)),
                  pl.BlockSpec((SB_W, d), lambda i: (0, 0)),
                  pl.BlockSpec((NSA_W, d), lambda i: (1, 0)),
                  row(d), mod_spec, const((1, d)), mod_spec, mod_spec],
        out_specs=[row(d), row(d)],
        compiler_params=_cparams(("arbitrary",)),
        name="out_projection",
    )(o_sb, o_nsa, g_sb, g_nsa, w_out_bf, w_out_bf, x2d, gate, g2, shift, scale)


def _peer_score_kernel(h_ref, wq_ref, k1_ref, k2_ref, s_ref):
    q = _dot(h_ref[...], wq_ref[...]).astype(BF16)
    half = D_QUERY // 2
    nchunk = h_ref.shape[0] // LANES
    for h in range(PEER_HEADS):
        for c, k_ref in enumerate((k1_ref, k2_ref)):
            qh = q[:, h * D_QUERY + c * half:h * D_QUERY + (c + 1) * half]
            st = _dot_nt(k_ref[h], qh)
            for ch in range(nchunk):
                s_ref[c, h, ch] = st[:, ch * LANES:(ch + 1) * LANES]


def peer_scores(h2, wq_bf, k1_bf, k2_bf):
    r, d = h2.shape
    tm = PEER_TN
    nchunk = tm // LANES
    return pl.pallas_call(
        _peer_score_kernel,
        out_shape=jax.ShapeDtypeStruct((2, PEER_HEADS, r // LANES, N_KEYS, LANES), F32),
        grid=(r // tm,),
        in_specs=[pl.BlockSpec((tm, d), lambda i: (i, 0)),
                  pl.BlockSpec(wq_bf.shape, lambda i: (0, 0)),
                  pl.BlockSpec(k1_bf.shape, lambda i: (0, 0, 0)),
                  pl.BlockSpec(k2_bf.shape, lambda i: (0, 0, 0))],
        out_specs=pl.BlockSpec((2, PEER_HEADS, nchunk, N_KEYS, LANES), lambda i: (0, 0, i, 0, 0)),
        compiler_params=_cparams(("arbitrary",)),
        name="peer_scores",
    )(h2, wq_bf, k1_bf, k2_bf)


def _top_values(x, n, scr):
    for it in range(n):
        m = jnp.max(x, axis=0, keepdims=True)
        scr[it:it + 1, :] = m
        x = jnp.where(x == m, NEG_INF, x)
    return scr[0:n, :]


def _peer_select_kernel(s_ref, s1_ref, s2_ref, tau_ref, v1_scr, v2_scr):
    k = PEER_TOPK
    nchunk = s_ref.shape[2]

    def body(idx, _):
        h, ch = idx // nchunk, idx % nchunk
        s1 = s_ref[0, h, ch]
        s2 = s_ref[1, h, ch]
        v1 = _top_values(s1, k, v1_scr)
        v2 = _top_values(s2, k, v2_scr)
        t1, t2 = v1[k - 1:k], v2[k - 1:k]
        def cands(a1):
            out = [a1[0:1] + v2]
            out += [a1[i:i + 1] + v2[0:k // 2] for i in range(1, k // 2)]
            out += [a1[k // 2:k] + v2[0:1]]
            return jnp.concatenate(out, axis=0)
        x = cands(v1)
        xs = x
        for _ in range(k - 1):
            xs = jnp.where(xs == jnp.max(xs, axis=0, keepdims=True), NEG_INF, xs)
        tau = jnp.max(xs, axis=0, keepdims=True)
        vmax = v1[0:1] + v2[0:1]
        chosen = x >= tau
        z = jnp.sum(jnp.where(chosen, jnp.exp(x - vmax), 0.0), axis=0, keepdims=True)
        shift = vmax + jnp.log(z)
        y = cands(v1 - shift)
        tau_y = jnp.min(jnp.where(chosen, y, -NEG_INF), axis=0, keepdims=True)
        s1_ref[h, ch] = jnp.where(s1 >= t1, s1 - shift, NEG_INF)
        s2_ref[h, ch] = jnp.where(s2 >= t2, s2, NEG_INF)
        tau_ref[h, ch] = jnp.broadcast_to(tau_y, (SUBLANES, LANES))
        return 0

    lax.fori_loop(0, PEER_HEADS * nchunk, body, 0)


def peer_select(s):
    _, heads, nch_all, keys, lanes = s.shape
    nchunk = PEER_TN // LANES
    blk = (heads, nchunk, keys, lanes)
    return pl.pallas_call(
        _peer_select_kernel,
        out_shape=[jax.ShapeDtypeStruct((heads, nch_all, keys, lanes), F32),
                   jax.ShapeDtypeStruct((heads, nch_all, keys, lanes), F32),
                   jax.ShapeDtypeStruct((heads, nch_all, SUBLANES, lanes), F32)],
        grid=(nch_all // nchunk,),
        in_specs=[pl.BlockSpec((2,) + blk, lambda i: (0, 0, i, 0, 0))],
        out_specs=[pl.BlockSpec(blk, lambda i: (0, i, 0, 0)),
                   pl.BlockSpec(blk, lambda i: (0, i, 0, 0)),
                   pl.BlockSpec((heads, nchunk, SUBLANES, lanes), lambda i: (0, i, 0, 0))],
        scratch_shapes=[pltpu.VMEM((PEER_TOPK, LANES), F32), pltpu.VMEM((PEER_TOPK, LANES), F32)],
        compiler_params=_cparams(("arbitrary",)),
        name="peer_select",
    )(s)


def _peer_expert_kernel(h_ref, u_ref, vt_ref, s1_ref, s2_ref, tau_ref, o_ref, ga_scr):
    e = pl.program_id(1)

    @pl.when(e == 0)
    def _():
        o_ref[...] = jnp.zeros_like(o_ref)

    act = _gelu(_dot_nt(u_ref[...], h_ref[...]))
    nchunk = h_ref.shape[0] // LANES
    for ch in range(nchunk):
        for al in range(PEER_AT):
            a = e * PEER_AT + al
            gsum = jnp.zeros((N_KEYS, LANES), F32)
            for h in range(PEER_HEADS):
                y = s1_ref[h, ch, pl.ds(a, 1), :] + s2_ref[h, ch]
                gsum = gsum + jnp.where(y >= tau_ref[h, ch, 0:1, :], jnp.exp(y), 0.0)
            blk = act[al * N_KEYS:(al + 1) * N_KEYS, ch * LANES:(ch + 1) * LANES] * gsum
            ga_scr[al * N_KEYS:(al + 1) * N_KEYS, ch * LANES:(ch + 1) * LANES] = blk.astype(BF16)
    o_ref[...] += _dot(vt_ref[...], ga_scr[...])


def peer_experts(h2, u_bf, vt_bf, s1m, s2m, tau):
    r, d = h2.shape
    n_exp = u_bf.shape[0]
    tn = PEER_TN
    nchunk = tn // LANES
    et = PEER_AT * N_KEYS
    sblk = (PEER_HEADS, nchunk, N_KEYS, LANES)
    return pl.pallas_call(
        _peer_expert_kernel,
        out_shape=jax.ShapeDtypeStruct((d, r), F32),
        grid=(r // tn, n_exp // et),
        in_specs=[pl.BlockSpec((tn, d), lambda t, e: (t, 0)),
                  pl.BlockSpec((et, d), lambda t, e: (e, 0)),
                  pl.BlockSpec((d, et), lambda t, e: (0, e)),
                  pl.BlockSpec(sblk, lambda t, e: (0, t, 0, 0)),
                  pl.BlockSpec(sblk, lambda t, e: (0, t, 0, 0)),
                  pl.BlockSpec((PEER_HEADS, nchunk, SUBLANES, LANES), lambda t, e: (0, t, 0, 0))],
        out_specs=pl.BlockSpec((d, tn), lambda t, e: (0, t)),
        scratch_shapes=[pltpu.VMEM((et, tn), BF16)],
        compiler_params=_cparams(("arbitrary", "arbitrary")),
        name="peer_experts",
    )(h2, u_bf, vt_bf, s1m, s2m, tau)


def _final_kernel(x1_ref, pt_ref, gate_ref, g_ref, y_ref):
    x2 = x1_ref[...] + gate_ref[0] * pt_ref[...].T
    y_ref[...] = _rms(x2, g_ref[...])


def final_norm(x1, peer_t, col0, gate, g, tm, mod_map):
    r, d = x1.shape
    mod_block = (1,) + gate.shape[1:]
    return pl.pallas_call(
        _final_kernel,
        out_shape=jax.ShapeDtypeStruct((r, d), F32),
        grid=(r // tm,),
        in_specs=[pl.BlockSpec((tm, d), lambda i: (i, 0)),
                  pl.BlockSpec((d, tm), lambda i: (0, col0 // tm + i)),
                  pl.BlockSpec(mod_block, lambda i: (mod_map(i), 0, 0)),
                  pl.BlockSpec((1, d), lambda i: (0, 0))],
        out_specs=pl.BlockSpec((tm, d), lambda i: (i, 0)),
        compiler_params=_cparams(("arbitrary",)),
        name="final_norm",
    )(x1, peer_t, gate, g)


def _layer(x_prompt, x_sample, cache_sb, cache_cmp, cache_slc, state_win, page_table, c_prompt,
           c_sample, w_ada, b_ada, norm1_g, w_in, cmp_pe_k, cmp_w1_k, cmp_w2_k, cmp_pe_v, cmp_w1_v,
           cmp_w2_v, out_g_sb, out_g_nsa, w_out, norm2_g, peer_wq, peer_k1, peer_k2, peer_u, peer_v):
    b, t, d = x_prompt.shape
    bs, nq, _ = x_sample.shape
    n_pages, page = page_table.shape[1], cache_sb.shape[1]
    past = n_pages * page
    rp, rs = b * t, bs * nq

    w_in_bf = jnp.pad(w_in, ((0, 0), (0, IN_NT * IN_TN - D_IN))).astype(BF16)
    w_out_bf = w_out.astype(BF16)
    wq_bf = peer_wq.astype(BF16)
    k1_bf, k2_bf = peer_k1.astype(BF16), peer_k2.astype(BF16)
    u_bf = peer_u.astype(BF16)
    vt_bf = peer_v.astype(BF16).T
    pe = jnp.stack([cmp_pe_k, cmp_pe_v])
    w1 = jnp.stack([cmp_w1_k, cmp_w1_v]).astype(BF16)
    w2 = jnp.stack([cmp_w2_k, cmp_w2_v]).astype(BF16)
    row = lambda v: v.reshape(1, -1)

    n_c = b + bs
    c_all = jnp.pad(jnp.concatenate([c_prompt, c_sample]), ((0, -n_c % 16), (0, 0))).astype(BF16)
    mod = adaln_mod(c_all, w_ada, row(b_ada))
    mod_p = [mod[:b, k * d:(k + 1) * d].reshape(b, 1, d) for k in range(6)]
    mod_s = [jnp.repeat(mod[b:n_c, k * d:(k + 1) * d], nq, axis=0).reshape(1, rs, d) for k in range(6)]

    tm_p = 1024 if t % 1024 == 0 else t
    xp = x_prompt.reshape(rp, d)
    tabs_p = _rope_tables(jnp.arange(t))
    qsb_p, kvsb_p, qnsa_p, kvcmp_p, kvslc_p, kvwin_p, gates_p = in_projection(
        xp, row(norm1_g), mod_p[0], mod_p[1], w_in_bf, tabs_p, tm_p,
        lambda i: i // (t // tm_p), lambda i: i % (t // tm_p))
    osb_p = sb_prompt(qsb_p, kvsb_p, b, t)
    kvc_p = compress_prompt(kvcmp_p, pe, w1, w2, b, t)
    onsa_p = nsa_prompt(qnsa_p, kvc_p, kvslc_p, kvwin_p, gates_p, b, t)
    tm_o = 256
    x1_p, h2_p = out_projection(osb_p, onsa_p, row(out_g_sb), row(out_g_nsa), w_out_bf, xp,
                                mod_p[2], row(norm2_g), mod_p[3], mod_p[4], tm_o,
                                lambda i: i // (t // tm_o))

    xs = x_sample.reshape(rs, d)
    tabs_s = _rope_tables(past + jnp.tile(jnp.arange(nq), bs))
    qsb_s, kvsb_s, qnsa_s, kvcmp_s, kvslc_s, kvwin_s, gates_s = in_projection(
        xs, row(norm1_g), mod_s[0], mod_s[1], w_in_bf, tabs_s, rs, lambda i: 0, lambda i: 0)
    osb_s = sb_sample(cache_sb, page_table, qsb_s.reshape(bs, nq, SB_W), kvsb_s.reshape(bs, nq, 2 * SB_W))
    kvcmp_s3 = kvcmp_s.reshape(bs, nq, KV_W)
    kvc_s = compress_sample(cache_cmp, page_table, kvcmp_s3, pe, w1, w2)
    pad8 = lambda a: jnp.pad(a.reshape(bs, nq, KV_W), ((0, 0), (0, -nq % SUBLANES), (0, 0)))
    qnsa_s3 = qnsa_s.reshape(bs, nq, NSA_W)
    st_win = state_win.reshape(bs, state_win.shape[1], KV_W)
    ocmp_s, owin_s, selx = nsa_sample_select(qnsa_s3, kvc_s, st_win, pad8(kvwin_s), past)
    onsa_s = nsa_sample_slc(cache_slc, page_table, qnsa_s3, pad8(kvslc_s), selx, ocmp_s, owin_s,
                            gates_s.reshape(bs, nq, IN_TN))
    x1_s, h2_s = out_projection(osb_s.reshape(rs, SB_W), onsa_s.reshape(rs, NSA_W), row(out_g_sb),
                                row(out_g_nsa), w_out_bf, xs, mod_s[2], row(norm2_g), mod_s[3],
                                mod_s[4], rs, lambda i: 0)

    r = rp + rs
    r_pad = -(-r // PEER_TN) * PEER_TN
    h2 = jnp.pad(jnp.concatenate([h2_p, h2_s]), ((0, r_pad - r), (0, 0)))
    s = peer_scores(h2, wq_bf, k1_bf, k2_bf)
    s1m, s2m, tau = peer_select(s)
    peer_t = peer_experts(h2, u_bf, vt_bf, s1m, s2m, tau)

    states_p = (kvsb_p.reshape(b, t, 2, H_SB, HEAD_DIM), kvcmp_p.reshape(b, t, 2, G_NSA, HEAD_DIM),
                kvslc_p.reshape(b, t, 2, G_NSA, HEAD_DIM),
                kvwin_p.reshape(b, t, 2, G_NSA, HEAD_DIM)[:, t - min(WINDOW, t):])
    kvwin_s5 = kvwin_s.reshape(bs, nq, 2, G_NSA, HEAD_DIM)
    states_s = (kvsb_s.reshape(bs, nq, 2, H_SB, HEAD_DIM), kvcmp_s.reshape(bs, nq, 2, G_NSA, HEAD_DIM),
                kvslc_s.reshape(bs, nq, 2, G_NSA, HEAD_DIM),
                jnp.concatenate([state_win, kvwin_s5], axis=1)[:, nq:])
    return (x1_p, mod_p[5], x1_s, mod_s[5], peer_t, rp), states_p, states_s


def kernel(x_prompt, x_sample, cache_sb, cache_cmp, cache_slc, state_win, page_table, c_prompt, c_sample, w_ada, b_ada, norm1_g, w_in, cmp_pe_k, cmp_w1_k, cmp_w2_k, cmp_pe_v, cmp_w1_v, cmp_w2_v, out_g_sb, out_g_nsa, w_out, norm2_g, peer_wq, peer_k1, peer_k2, peer_u, peer_v, final_g):
    depth = w_ada.shape[0]
    assert depth == 1, "single-layer trunk"
    b, t, d = x_prompt.shape
    bs, nq, _ = x_sample.shape
    (x1_p, gate_p, x1_s, gate_s, peer_t, rp), st_p, st_s = _layer(
        x_prompt, x_sample, cache_sb[0], cache_cmp[0], cache_slc[0], state_win[0], page_table,
        c_prompt, c_sample, w_ada[0], b_ada[0], norm1_g[0], w_in[0], cmp_pe_k[0], cmp_w1_k[0],
        cmp_w2_k[0], cmp_pe_v[0], cmp_w1_v[0], cmp_w2_v[0], out_g_sb[0], out_g_nsa[0], w_out[0],
        norm2_g[0], peer_wq[0], peer_k1[0], peer_k2[0], peer_u[0], peer_v[0])
    fg = final_g.reshape(1, d)
    tm_f = 256
    y_p = final_norm(x1_p, peer_t, 0, gate_p, fg, tm_f, lambda i: i // (t // tm_f)).reshape(b, t, d)
    rs = bs * nq
    y_s = final_norm(x1_s, peer_t, rp, gate_s, fg, rs, lambda i: 0).reshape(bs, nq, d)
    return (y_p, y_s, st_p[0][None], st_s[0][None], st_p[1][None], st_s[1][None],
            st_p[2][None], st_s[2][None], st_p[3][None], st_s[3][None])
```

```python
import functools

import jax
import jax.numpy as jnp
from jax import lax
from jax.experimental import pallas as pl
from jax.experimental.pallas import tpu as pltpu

F32 = jnp.float32
BF16 = jnp.bfloat16

HEAD_DIM = 128
H_SB = 8
H_NSA = 8
G_NSA = 2
HPG = H_NSA // G_NSA
SB_W = H_SB * HEAD_DIM
NSA_W = H_NSA * HEAD_DIM
KV_W = 2 * G_NSA * HEAD_DIM
D_IN = 3 * SB_W + NSA_W + 3 * KV_W + 3 * H_NSA
SCALE = HEAD_DIM ** -0.5
ROT_HALF = HEAD_DIM // 8
ROPE_THETA = 500000.0
CMP_BLK = 64
TOP_BLOCKS = 16
WINDOW = 512
FORCE_SCORE = 1e9
N_KEYS = 128
PEER_HEADS = 8
PEER_TOPK = 16
D_QUERY = 256
EPS = 1e-6
NEG_INF = -1e30
LOG2E = 1.4426950408889634

LANES = 128
SUBLANES = 8
VMEM_LIMIT = 56 * 1024 * 1024

MXU_N = 256
IN_TN = 2 * MXU_N
IN_NT = -(-D_IN // IN_TN)
GATE_W = LANES
PEER_TN = 3 * MXU_N
CMP_PITCH = CMP_BLK * 2 * G_NSA + SUBLANES
PEER_AT = 4
PAGES_PER_STEP_SB = 8
PAGES_PER_STEP_NSA = 16


def _cparams(sem):
    return pltpu.CompilerParams(dimension_semantics=sem, vmem_limit_bytes=VMEM_LIMIT)


def _dot(a, b):
    return jnp.dot(a, b, preferred_element_type=F32)


def _dot_nt(a, b):
    return lax.dot_general(a, b, (((1,), (1,)), ((), ())), preferred_element_type=F32)


def _dot_tn(a, b):
    return lax.dot_general(a, b, (((0,), (0,)), ((), ())), preferred_element_type=F32)


def _gelu(x):
    return 0.5 * x * (1.0 + jnp.tanh(0.7978845608028654 * (x + 0.044715 * (x * x * x))))


def _rms(x, g):
    return x * lax.rsqrt(jnp.mean(x * x, axis=-1, keepdims=True) + EPS) * g


def _softplus(z):
    return jnp.maximum(z, 0.0) + jnp.log(1.0 + jnp.exp(-jnp.abs(z)))


def _split_bf16(x):
    hi = x.astype(BF16)
    lo = (x - hi.astype(F32)).astype(BF16)
    return hi, lo


def _mod_kernel(c_ref, w_ref, b_ref, o_ref):
    o_ref[...] = _dot(c_ref[...], w_ref[...].astype(BF16)) + b_ref[...]


def adaln_mod(c_bf, w_ada, b_ada):
    m, d = c_bf.shape
    n = w_ada.shape[1]
    tn = 1024
    return pl.pallas_call(
        _mod_kernel,
        out_shape=jax.ShapeDtypeStruct((m, n), F32),
        grid=(n // tn,),
        in_specs=[pl.BlockSpec((m, d), lambda j: (0, 0)),
                  pl.BlockSpec((d, tn), lambda j: (0, j)),
                  pl.BlockSpec((1, tn), lambda j: (0, j))],
        out_specs=pl.BlockSpec((m, tn), lambda j: (0, j)),
        compiler_params=_cparams(("arbitrary",)),
        name="adaln_mod",
    )(c_bf, w_ada, b_ada)


_IN_SEGMENTS = (
    (0, 2, 0, "plain"),
    (2, 6, 1, "plain"),
    (6, 8, 2, "rope"),
    (8, 9, 3, "rope_k"),
    (9, 10, 4, "rope_k"),
    (10, 11, 5, "rope_k"),
    (11, 12, 6, "gate"),
)
_IN_OUT_RANGES = ((0, 2), (2, 6), (6, 8), (8, 9), (9, 10), (10, 11), (11, 12))


def _rope(p, rc, ra, rb):
    w = p.shape[1]
    return p * rc + pltpu.roll(p, ROT_HALF, 1) * ra + pltpu.roll(p, w - ROT_HALF, 1) * rb


def _inproj_kernel(x_ref, g_ref, shift_ref, scale_ref, w_ref, rc_ref, ra_ref, rb_ref,
                   *rest):
    outs, h_scr = rest[:7], rest[7]
    j = pl.program_id(1)

    @pl.when(j == 0)
    def _():
        h = _rms(x_ref[...], g_ref[...]) * (1.0 + scale_ref[0]) + shift_ref[0]
        h_scr[...] = h.astype(BF16)

    p = _dot(h_scr[...], w_ref[...])
    kw = G_NSA * HEAD_DIM
    for lo, hi, slot, kind in _IN_SEGMENTS:
        @pl.when((j >= lo) & (j < hi))
        def _(slot=slot, kind=kind):
            if kind == "rope":
                v = _rope(p, rc_ref[...], ra_ref[...], rb_ref[...])
            elif kind == "rope_k":
                v = jnp.concatenate([_rope(p[:, :kw], rc_ref[:, :kw], ra_ref[:, :kw], rb_ref[:, :kw]),
                                     p[:, kw:]], axis=1)
            elif kind == "gate":
                v = jax.nn.sigmoid(p[:, :GATE_W])
            else:
                v = p
            outs[slot][...] = v.astype(outs[slot].dtype)


def _rope_tables(pos):
    inv = jnp.power(ROPE_THETA, -jnp.arange(ROT_HALF, dtype=F32) / ROT_HALF)
    ang = pos.astype(F32)[:, None] * inv[None, :]
    cos, sin = jnp.cos(ang), jnp.sin(ang)
    n = pos.shape[0]
    z16 = jnp.zeros((n, ROT_HALF), F32)
    tail0 = jnp.zeros((n, HEAD_DIM - 2 * ROT_HALF), F32)
    rc = jnp.concatenate([cos, cos, tail0 + 1.0], axis=1)
    ra = jnp.concatenate([z16, sin, tail0], axis=1)
    rb = jnp.concatenate([-sin, z16, tail0], axis=1)
    heads = IN_TN // HEAD_DIM
    return tuple(jnp.tile(t, (1, heads)) for t in (rc, ra, rb))


def in_projection(x2d, g1, shift, scale, w_bf, tables, tm, mod_map, tab_map):
    r, d = x2d.shape
    widths = [(hi - lo) * IN_TN for lo, hi in _IN_OUT_RANGES[:-1]] + [GATE_W]
    dtypes = [BF16, F32, BF16, F32, F32, F32, F32]
    out_shape = [jax.ShapeDtypeStruct((r, w), dt) for w, dt in zip(widths, dtypes)]
    blocks = [IN_TN] * 6 + [GATE_W]

    def out_map(lo, hi):
        return lambda i, j: (i, jnp.clip(j - lo, 0, hi - lo - 1))

    mod_block = (1,) + shift.shape[1:]
    return pl.pallas_call(
        _inproj_kernel,
        out_shape=out_shape,
        grid=(r // tm, IN_NT),
        in_specs=[pl.BlockSpec((tm, d), lambda i, j: (i, 0)),
                  pl.BlockSpec((1, d), lambda i, j: (0, 0)),
                  pl.BlockSpec(mod_block, lambda i, j: (mod_map(i), 0, 0)),
                  pl.BlockSpec(mod_block, lambda i, j: (mod_map(i), 0, 0)),
                  pl.BlockSpec((d, IN_TN), lambda i, j: (0, j)),
                  pl.BlockSpec((tm, IN_TN), lambda i, j: (tab_map(i), 0)),
                  pl.BlockSpec((tm, IN_TN), lambda i, j: (tab_map(i), 0)),
                  pl.BlockSpec((tm, IN_TN), lambda i, j: (tab_map(i), 0))],
        out_specs=[pl.BlockSpec((tm, w), out_map(lo, hi))
                   for w, (lo, hi) in zip(blocks, _IN_OUT_RANGES)],
        scratch_shapes=[pltpu.VMEM((tm, d), BF16)],
        compiler_params=_cparams(("arbitrary", "arbitrary")),
        name="in_projection",
    )(x2d, g1, shift, scale, w_bf, *tables)


def _sb_weights(z, valid, later, tri):
    rows, keys = z.shape
    sub = tri.shape[0]
    nsub = keys // sub
    sp = _softplus(z)
    log1m = -sp if valid is None else jnp.where(valid, -sp, 0.0)
    stacked = jnp.concatenate([log1m[:, i * sub:(i + 1) * sub] for i in range(nsub)], axis=0)
    hi, lo = _split_bf16(stacked)
    within = _dot(hi, tri) + _dot(lo, tri)
    afters = [None] * nsub
    for i in reversed(range(nsub)):
        w = within[i * rows:(i + 1) * rows]
        afters[i] = w + later
        later = later + w[:, 0:1] + log1m[:, i * sub:i * sub + 1]
    a = jnp.exp(z - sp + jnp.concatenate(afters, axis=1))
    if valid is not None:
        a = jnp.where(valid, a, 0.0)
    return a, later


def _later_keys_tri(sub):
    return (lax.broadcasted_iota(jnp.int32, (sub, sub), 0)
            > lax.broadcasted_iota(jnp.int32, (sub, sub), 1)).astype(BF16)


def _sb_prompt_kernel(q_ref, k_ref, v_ref, o_ref, *, tq, tk):
    qi = pl.program_id(2)
    q = q_ref[...]
    qpos = qi * tq + lax.broadcasted_iota(jnp.int32, (tq, tk), 0)
    lane = lax.broadcasted_iota(jnp.int32, (tq, tk), 1)
    tri = _later_keys_tri(LANES)
    nk = ((qi + 1) * tq + tk - 1) // tk

    def body(it, carry):
        later, acc = carry
        start = pl.multiple_of((nk - 1 - it) * tk, tk)
        k = k_ref[pl.ds(start, tk), :].astype(BF16)
        v = v_ref[pl.ds(start, tk), :].astype(BF16)
        z = _dot_nt(q, k) * SCALE
        a, later = _sb_weights(z, (start + lane) < qpos, later, tri)
        return later, acc + _dot(a.astype(BF16), v)

    _, acc = lax.fori_loop(0, nk, body, (jnp.zeros((tq, 1), F32), jnp.zeros((tq, HEAD_DIM), F32)))
    o_ref[...] = acc


def sb_prompt(q_sb, kv_sb, b, t):
    tq = 256
    tk = 512 if t % 512 == 0 else tq
    nq = t // tq
    return pl.pallas_call(
        functools.partial(_sb_prompt_kernel, tq=tq, tk=tk),
        out_shape=jax.ShapeDtypeStruct((b * t, SB_W), F32),
        grid=(b, H_SB, nq),
        in_specs=[pl.BlockSpec((tq, HEAD_DIM), lambda bi, h, qi: (bi * nq + qi, h)),
                  pl.BlockSpec((t, HEAD_DIM), lambda bi, h, qi: (bi, h)),
                  pl.BlockSpec((t, HEAD_DIM), lambda bi, h, qi: (bi, H_SB + h))],
        out_specs=pl.BlockSpec((tq, HEAD_DIM), lambda bi, h, qi: (bi * nq + qi, h)),
        compiler_params=_cparams(("arbitrary", "arbitrary", "arbitrary")),
        name="sb_prompt",
    )(q_sb, kv_sb, kv_sb)


def _compress_rows(read_rows, pe_ref, w1_ref, w2_ref, x_scr, c):
    for p in range(CMP_BLK):
        x = jnp.concatenate([read_rows(p, 0), read_rows(p, 1)], axis=0) + pe_ref[c, p:p + 1, :]
        x_scr[:, p * HEAD_DIM:(p + 1) * HEAD_DIM] = x.astype(BF16)
    hid = _gelu(_dot(x_scr[...], w1_ref[c]))
    return _dot(hid.astype(BF16), w2_ref[c])


def _cmp_prompt_kernel(x_ref, pe_ref, w1_ref, w2_ref, o_ref, x_scr, *, nb):
    stride = CMP_BLK * 2 * G_NSA
    for c in range(2):
        def read_rows(p, g, c=c):
            return x_ref[0, pl.ds(p * 2 * G_NSA + c * G_NSA + g, nb, stride=stride), :]
        out = _compress_rows(read_rows, pe_ref, w1_ref, w2_ref, x_scr, c)
        o_ref[0, c, 0] = out[:nb]
        o_ref[0, c, 1] = out[nb:]


def compress_prompt(kv_cmp, pe, w1, w2, b, t):
    nb = t // CMP_BLK
    x = kv_cmp.reshape(b, t * 2 * G_NSA, HEAD_DIM)
    return pl.pallas_call(
        functools.partial(_cmp_prompt_kernel, nb=nb),
        out_shape=jax.ShapeDtypeStruct((b, 2, G_NSA, nb, HEAD_DIM), F32),
        grid=(b,),
        in_specs=[pl.BlockSpec((1, t * 2 * G_NSA, HEAD_DIM), lambda i: (i, 0, 0)),
                  pl.BlockSpec(pe.shape, lambda i: (0, 0, 0)),
                  pl.BlockSpec(w1.shape, lambda i: (0, 0, 0)),
                  pl.BlockSpec(w2.shape, lambda i: (0, 0, 0))],
        out_specs=pl.BlockSpec((1, 2, G_NSA, nb, HEAD_DIM), lambda i: (i, 0, 0, 0, 0)),
        scratch_shapes=[pltpu.VMEM((G_NSA * nb, CMP_BLK * HEAD_DIM), BF16)],
        compiler_params=_cparams(("arbitrary",)),
        name="compress_prompt",
    )(x, pe, w1, w2)


def _cmp_sample_kernel(pt_ref, *refs, pages, nbp, past):
    page_refs = refs[:pages]
    new_ref, pe_ref, w1_ref, w2_ref, o_ref, buf, x_scr = refs[pages:]
    b, jb = pl.program_id(0), pl.program_id(1)
    blk_rows = CMP_BLK * 2 * G_NSA
    blks_per_page = page_refs[0].shape[1] // blk_rows
    past_blks = past // CMP_BLK

    @pl.when((b == 0) & (jb == 0))
    def _():
        tail = buf.shape[0] - past_blks * CMP_PITCH
        buf[pl.ds(past_blks * CMP_PITCH, tail), :] = jnp.zeros((tail, HEAD_DIM), F32)

    for i, pr in enumerate(page_refs):
        for hb in range(blks_per_page):
            blk = (jb * pages + i) * blks_per_page + hb
            start = pl.multiple_of(blk * CMP_PITCH, SUBLANES)
            buf[pl.ds(start, blk_rows), :] = pr[0, hb * blk_rows:(hb + 1) * blk_rows, :]

    @pl.when(jb == pl.num_programs(1) - 1)
    def _():
        buf[pl.ds(past_blks * CMP_PITCH, new_ref.shape[1]), :] = new_ref[0]
        for c in range(2):
            def read_rows(p, g, c=c):
                return buf[pl.ds(p * 2 * G_NSA + c * G_NSA + g, nbp, stride=CMP_PITCH), :]
            out = _compress_rows(read_rows, pe_ref, w1_ref, w2_ref, x_scr, c)
            o_ref[0, c, 0] = out[:nbp]
            o_ref[0, c, 1] = out[nbp:]


def compress_sample(cache_cmp, page_table, kv_new, pe, w1, w2):
    n_pool, page = cache_cmp.shape[:2]
    bsz, n_pages = page_table.shape
    nq = kv_new.shape[1]
    past = n_pages * page
    nb = -(-(past + nq) // CMP_BLK)
    nbp = -(-nb // SUBLANES) * SUBLANES
    pages = min(PAGES_PER_STEP_NSA, n_pages)
    rpp = page * 2 * G_NSA
    cache = cache_cmp.reshape(n_pool, rpp, HEAD_DIM)
    new = kv_new.reshape(bsz, nq * 2 * G_NSA, HEAD_DIM)

    def page_map(i):
        return lambda b, jb, pt: (pt[b, jb * pages + i], 0, 0)

    grid_spec = pltpu.PrefetchScalarGridSpec(
        num_scalar_prefetch=1,
        grid=(bsz, n_pages // pages),
        in_specs=[pl.BlockSpec((1, rpp, HEAD_DIM), page_map(i)) for i in range(pages)]
        + [pl.BlockSpec((1,) + new.shape[1:], lambda b, jb, pt: (b, 0, 0)),
           pl.BlockSpec(pe.shape, lambda b, jb, pt: (0, 0, 0)),
           pl.BlockSpec(w1.shape, lambda b, jb, pt: (0, 0, 0)),
           pl.BlockSpec(w2.shape, lambda b, jb, pt: (0, 0, 0))],
        out_specs=pl.BlockSpec((1, 2, G_NSA, nbp, HEAD_DIM), lambda b, jb, pt: (b, 0, 0, 0, 0)),
        scratch_shapes=[pltpu.VMEM((nbp * CMP_PITCH, HEAD_DIM), F32),
                        pltpu.VMEM((G_NSA * nbp, CMP_BLK * HEAD_DIM), BF16)],
    )
    return pl.pallas_call(
        functools.partial(_cmp_sample_kernel, pages=pages, nbp=nbp, past=past),
        out_shape=jax.ShapeDtypeStruct((bsz, 2, G_NSA, nbp, HEAD_DIM), F32),
        grid_spec=grid_spec,
        compiler_params=_cparams(("arbitrary", "arbitrary")),
        name="compress_sample",
    )(page_table, *([cache] * pages), new, pe, w1, w2)


def _cmp_branch(q4, qpos4, kc, vc, nq, nbp):
    blk = lax.broadcasted_iota(jnp.int32, (1, nbp), 1)
    s = _dot_nt(q4, kc.astype(BF16)) * SCALE
    complete = (blk + 1) * CMP_BLK <= qpos4 + 1
    m = jnp.max(jnp.where(complete, s, NEG_INF), axis=1, keepdims=True)
    e = jnp.where(complete, jnp.exp(s - m), 0.0)
    den = jnp.sum(e, axis=1, keepdims=True)
    p = e / jnp.where(den > 0.0, den, 1.0)
    o_cmp = _dot(p.astype(BF16), vc.astype(BF16))
    imp = p[0:nq]
    for n in range(1, HPG):
        imp = imp + p[n * nq:(n + 1) * nq]
    qpos = qpos4[0:nq]
    cur = qpos // CMP_BLK
    started = blk <= cur
    forced = started & ((blk == 0) | (blk >= cur - 1))
    score = jnp.where(forced, FORCE_SCORE, jnp.where(started, imp, -FORCE_SCORE))
    rank = jnp.zeros((nq, nbp), F32)
    for j in range(nbp):
        col = score[:, j:j + 1]
        rank = rank + jnp.where(col > score, 1.0, 0.0) + jnp.where((col == score) & (blk > j), 1.0, 0.0)
    return o_cmp, rank


def _flash_tile(q4, k, v, mask, state):
    m, l, acc = state
    s = _dot_nt(q4, k) * SCALE
    m_new = jnp.maximum(m, jnp.max(jnp.where(mask, s, NEG_INF), axis=1, keepdims=True))
    p = jnp.where(mask, jnp.exp(s - m_new), 0.0)
    alpha = jnp.exp(m - m_new)
    l = alpha * l + jnp.sum(p, axis=1, keepdims=True)
    acc = alpha * acc + _dot(p.astype(BF16), v)
    return m_new, l, acc


def _flash_init(rows):
    return (jnp.full((rows, 1), NEG_INF, F32), jnp.zeros((rows, 1), F32),
            jnp.zeros((rows, HEAD_DIM), F32))


def _flash_out(state):
    _, l, acc = state
    return acc / jnp.where(l > 0.0, l, 1.0)


def _stack_heads(q_ref):
    return jnp.concatenate([q_ref[:, n * HEAD_DIM:(n + 1) * HEAD_DIM] for n in range(HPG)], axis=0)


def _nsa_prompt_kernel(q_ref, kc_ref, vc_ref, sk_ref, sv_ref, wk_ref, wv_ref, gate_ref, o_ref,
                       selx_scr, *, tq, tk, nb, topk):
    g, qi = pl.program_id(1), pl.program_id(2)
    rows = HPG * tq
    q4 = _stack_heads(q_ref)
    t_in = lax.broadcasted_iota(jnp.int32, (tq, 1), 0)
    qpos1 = qi * tq + t_in
    qpos4 = jnp.concatenate([qpos1] * HPG, axis=0)
    o_cmp, rank = _cmp_branch(q4, qpos4, kc_ref[0, 0, 0], vc_ref[0, 0, 0], tq, nb)
    sel = jnp.where(rank < topk, 1.0, 0.0).astype(BF16)
    nkt = selx_scr.shape[0]
    kb = lax.broadcasted_iota(jnp.int32, (nb, tk), 0)
    kl = lax.broadcasted_iota(jnp.int32, (nb, tk), 1)
    for j in range(nkt):
        expand = jnp.where((j * tk + kl) // CMP_BLK == kb, 1.0, 0.0).astype(BF16)
        selx_scr[j] = _dot(sel, expand)
    lane = lax.broadcasted_iota(jnp.int32, (rows, tk), 1)

    def slc_body(j, state):
        start = pl.multiple_of(j * tk, tk)
        k = sk_ref[pl.ds(start, tk), :].astype(BF16)
        v = sv_ref[pl.ds(start, tk), :].astype(BF16)
        sx = selx_scr[j]
        mask = (jnp.concatenate([sx] * HPG, axis=0) > 0.5) & ((start + lane) <= qpos4)
        return _flash_tile(q4, k, v, mask, state)

    o_slc = _flash_out(lax.fori_loop(0, (qi * tq) // tk + tq // tk, slc_body, _flash_init(rows)))

    def win_body(j, state):
        start = pl.multiple_of(j * tk, tk)
        k = wk_ref[pl.ds(start, tk), :].astype(BF16)
        v = wv_ref[pl.ds(start, tk), :].astype(BF16)
        d = qpos4 - (start + lane)
        mask = (d >= 0) & (d < WINDOW)
        return _flash_tile(q4, k, v, mask, state)

    first = jnp.maximum(qi * tq - WINDOW, 0) // tk
    o_win = _flash_out(lax.fori_loop(first, (qi * tq) // tk + tq // tk, win_body, _flash_init(rows)))

    gates = gate_ref[...]

    def gate(n, branch):
        col = gates[:, n * 3 + branch:n * 3 + branch + 1]
        for gg in range(1, G_NSA):
            c = (gg * HPG + n) * 3 + branch
            col = jnp.where(g == gg, gates[:, c:c + 1], col)
        return col

    for n in range(HPG):
        r0 = n * tq
        o_ref[:, n * HEAD_DIM:(n + 1) * HEAD_DIM] = (
            gate(n, 0) * o_cmp[r0:r0 + tq] + gate(n, 1) * o_slc[r0:r0 + tq]
            + gate(n, 2) * o_win[r0:r0 + tq])


def nsa_prompt(q_nsa, kvc, kv_slc, kv_win, gates, b, t):
    tq = tk = 128
    nq = t // tq
    nb = t // CMP_BLK
    topk = min(TOP_BLOCKS, nb)
    gw = HPG * HEAD_DIM
    return pl.pallas_call(
        functools.partial(_nsa_prompt_kernel, tq=tq, tk=tk, nb=nb, topk=topk),
        out_shape=jax.ShapeDtypeStruct((b * t, NSA_W), F32),
        grid=(b, G_NSA, nq),
        in_specs=[pl.BlockSpec((tq, gw), lambda bi, g, qi: (bi * nq + qi, g)),
                  pl.BlockSpec((1, 1, 1, nb, HEAD_DIM), lambda bi, g, qi: (bi, 0, g, 0, 0)),
                  pl.BlockSpec((1, 1, 1, nb, HEAD_DIM), lambda bi, g, qi: (bi, 1, g, 0, 0)),
                  pl.BlockSpec((t, HEAD_DIM), lambda bi, g, qi: (bi, g)),
                  pl.BlockSpec((t, HEAD_DIM), lambda bi, g, qi: (bi, G_NSA + g)),
                  pl.BlockSpec((t, HEAD_DIM), lambda bi, g, qi: (bi, g)),
                  pl.BlockSpec((t, HEAD_DIM), lambda bi, g, qi: (bi, G_NSA + g)),
                  pl.BlockSpec((tq, GATE_W), lambda bi, g, qi: (bi * nq + qi, 0))],
        out_specs=pl.BlockSpec((tq, gw), lambda bi, g, qi: (bi * nq + qi, g)),
        scratch_shapes=[pltpu.VMEM((t // tk, tq, tk), F32)],
        compiler_params=_cparams(("arbitrary", "arbitrary", "arbitrary")),
        name="nsa_prompt",
    )(q_nsa, kvc, kvc, kv_slc, kv_slc, kv_win, kv_win, gates)


def _sb_sample_kernel(pt_ref, *refs, pages, nq):
    page_refs = refs[:pages]
    qt_ref, new_ref, o_ref, kbuf, vbuf, acc_scr, later_scr = refs[pages:]
    jb = pl.program_id(1)
    page = new_ref.shape[1]
    rows = H_SB * nq
    tri = _later_keys_tri(LANES)
    qt = qt_ref[0]

    def sweep(k, v, valid):
        z = _dot_nt(qt, k) * SCALE
        a, later = _sb_weights(z, valid, later_scr[...], tri)
        acc_scr[...] += _dot(a.astype(BF16), v)
        later_scr[...] = later

    @pl.when(jb == 0)
    def _():
        acc_scr[...] = jnp.zeros_like(acc_scr)
        later_scr[...] = jnp.zeros_like(later_scr)
        key = lax.broadcasted_iota(jnp.int32, (rows, page), 1)
        qry = lax.broadcasted_iota(jnp.int32, (rows, page), 0) % nq
        sweep(new_ref[0, :, :SB_W].astype(BF16), new_ref[0, :, SB_W:].astype(BF16), key < qry)

    stride = 2 * H_SB
    for i, pr in enumerate(page_refs):
        for h in range(H_SB):
            dst = (slice(i * page, (i + 1) * page), slice(h * HEAD_DIM, (h + 1) * HEAD_DIM))
            kbuf[dst] = pr[0, pl.ds(h, page, stride=stride), :].astype(BF16)
            vbuf[dst] = pr[0, pl.ds(H_SB + h, page, stride=stride), :].astype(BF16)
    sweep(kbuf[...], vbuf[...], None)

    @pl.when(jb == pl.num_programs(1) - 1)
    def _():
        acc = acc_scr[...]
        for h in range(H_SB):
            o_ref[0, :, h * HEAD_DIM:(h + 1) * HEAD_DIM] = acc[h * nq:(h + 1) * nq,
                                                               h * HEAD_DIM:(h + 1) * HEAD_DIM]


def sb_sample(cache_sb, page_table, q_sb, kv_new):
    n_pool, page = cache_sb.shape[:2]
    bsz, n_pages = page_table.shape
    nq = q_sb.shape[1]
    pages = min(PAGES_PER_STEP_SB, n_pages)
    nsteps = n_pages // pages
    prow = page * 2 * H_SB
    cache = cache_sb.reshape(n_pool, prow, HEAD_DIM)
    qh = q_sb.reshape(bsz, nq, H_SB, HEAD_DIM).transpose(0, 2, 1, 3)
    qt = (qh[:, :, :, None, :] * jnp.eye(H_SB, dtype=BF16)[None, :, None, :, None])
    qt = qt.reshape(bsz, H_SB * nq, SB_W)
    new_page = jnp.pad(kv_new, ((0, 0), (0, page - nq), (0, 0)))

    def page_map(i):
        return lambda b, jb, pt: (pt[b, n_pages - (jb + 1) * pages + i], 0, 0)

    grid_spec = pltpu.PrefetchScalarGridSpec(
        num_scalar_prefetch=1,
        grid=(bsz, nsteps),
        in_specs=[pl.BlockSpec((1, prow, HEAD_DIM), page_map(i)) for i in range(pages)]
        + [pl.BlockSpec((1, H_SB * nq, SB_W), lambda b, jb, pt: (b, 0, 0)),
           pl.BlockSpec((1, page, 2 * SB_W), lambda b, jb, pt: (b, 0, 0))],
        out_specs=pl.BlockSpec((1, nq, SB_W), lambda b, jb, pt: (b, 0, 0)),
        scratch_shapes=[pltpu.VMEM((pages * page, SB_W), BF16), pltpu.VMEM((pages * page, SB_W), BF16),
                        pltpu.VMEM((H_SB * nq, SB_W), F32), pltpu.VMEM((H_SB * nq, 1), F32)],
    )
    return pl.pallas_call(
        functools.partial(_sb_sample_kernel, pages=pages, nq=nq),
        out_shape=jax.ShapeDtypeStruct((bsz, nq, SB_W), F32),
        grid_spec=grid_spec,
        compiler_params=_cparams(("arbitrary", "arbitrary")),
        name="sb_sample",
    )(page_table, *([cache] * pages), qt, new_page)


def _nsa_sample_sel_kernel(q_ref, kc_ref, vc_ref, st_ref, new_ref, ocmp_ref, owin_ref, selx_ref,
                           *, nq, nbp, past, topk):
    rows = HPG * nq
    slots = 2 * G_NSA
    wb = st_ref.shape[1] // slots
    t_in = lax.broadcasted_iota(jnp.int32, (nq, 1), 0)
    qpos4 = jnp.concatenate([past + t_in] * HPG, axis=0)
    kb = lax.broadcasted_iota(jnp.int32, (nbp, nbp * CMP_BLK), 0)
    kl = lax.broadcasted_iota(jnp.int32, (nbp, nbp * CMP_BLK), 1)
    expand = jnp.where(kl // CMP_BLK == kb, 1.0, 0.0).astype(BF16)
    lane_w = lax.broadcasted_iota(jnp.int32, (rows, wb), 1)
    lane_n = lax.broadcasted_iota(jnp.int32, (rows, new_ref.shape[1]), 1)
    for g in range(G_NSA):
        q4 = jnp.concatenate([q_ref[0, :, (g * HPG + n) * HEAD_DIM:(g * HPG + n + 1) * HEAD_DIM]
                              for n in range(HPG)], axis=0)
        o_cmp, rank = _cmp_branch(q4, qpos4, kc_ref[0, 0, g], vc_ref[0, 0, g], nq, nbp)
        sel = jnp.where(rank < topk, 1.0, 0.0).astype(BF16)
        selx_ref[0, g] = _dot(sel, expand)
        kcol, vcol = g * HEAD_DIM, (G_NSA + g) * HEAD_DIM
        state = _flash_init(rows)
        d = qpos4 - (past - wb + lane_w)
        state = _flash_tile(q4, st_ref[0, pl.ds(g, wb, stride=slots), :].astype(BF16),
                            st_ref[0, pl.ds(G_NSA + g, wb, stride=slots), :].astype(BF16),
                            (d >= 0) & (d < WINDOW), state)
        d = qpos4 - (past + lane_n)
        state = _flash_tile(q4, new_ref[0, :, kcol:kcol + HEAD_DIM].astype(BF16),
                            new_ref[0, :, vcol:vcol + HEAD_DIM].astype(BF16),
                            (d >= 0) & (d < WINDOW) & (lane_n < nq), state)
        o_win = _flash_out(state)
        for n in range(HPG):
            h = g * HPG + n
            ocmp_ref[0, :, h * HEAD_DIM:(h + 1) * HEAD_DIM] = o_cmp[n * nq:(n + 1) * nq]
            owin_ref[0, :, h * HEAD_DIM:(h + 1) * HEAD_DIM] = o_win[n * nq:(n + 1) * nq]


def nsa_sample_select(q_nsa, kvc, state_win, win_new, past):
    bsz, nq = q_nsa.shape[:2]
    nbp = kvc.shape[3]
    nb = -(-(past + nq) // CMP_BLK)
    wrows = state_win.shape[1]
    return pl.pallas_call(
        functools.partial(_nsa_sample_sel_kernel, nq=nq, nbp=nbp, past=past, topk=min(TOP_BLOCKS, nb)),
        out_shape=[jax.ShapeDtypeStruct((bsz, nq, NSA_W), F32),
                   jax.ShapeDtypeStruct((bsz, nq, NSA_W), F32),
                   jax.ShapeDtypeStruct((bsz, G_NSA, nq, nbp * CMP_BLK), F32)],
        grid=(bsz,),
        in_specs=[pl.BlockSpec((1, nq, NSA_W), lambda b: (b, 0, 0)),
                  pl.BlockSpec((1, 1, G_NSA, nbp, HEAD_DIM), lambda b: (b, 0, 0, 0, 0)),
                  pl.BlockSpec((1, 1, G_NSA, nbp, HEAD_DIM), lambda b: (b, 1, 0, 0, 0)),
                  pl.BlockSpec((1, wrows, HEAD_DIM), lambda b: (b, 0, 0)),
                  pl.BlockSpec((1,) + win_new.shape[1:], lambda b: (b, 0, 0))],
        out_specs=[pl.BlockSpec((1, nq, NSA_W), lambda b: (b, 0, 0)),
                   pl.BlockSpec((1, nq, NSA_W), lambda b: (b, 0, 0)),
                   pl.BlockSpec((1, G_NSA, nq, nbp * CMP_BLK), lambda b: (b, 0, 0, 0))],
        compiler_params=_cparams(("arbitrary",)),
        name="nsa_sample_select",
    )(q_nsa, kvc, kvc, state_win, win_new)


def _nsa_sample_slc_kernel(pt_ref, *refs, pages, nq, past):
    page_refs = refs[:pages]
    (q_ref, new_ref, selx_ref, selx_new_ref, ocmp_ref, owin_ref, gate_ref, o_ref,
     kbuf, vbuf, m_scr, l_scr, acc_scr) = refs[pages:]
    jb = pl.program_id(1)
    rows = HPG * nq
    slots = 2 * G_NSA
    page = page_refs[0].shape[1] // slots
    t_in = lax.broadcasted_iota(jnp.int32, (nq, 1), 0)
    qpos4 = jnp.concatenate([past + t_in] * HPG, axis=0)
    q4s = [jnp.concatenate([q_ref[0, :, (g * HPG + n) * HEAD_DIM:(g * HPG + n + 1) * HEAD_DIM]
                            for n in range(HPG)], axis=0) for g in range(G_NSA)]

    @pl.when(jb == 0)
    def _():
        m_scr[...] = jnp.full(m_scr.shape, NEG_INF, F32)
        l_scr[...] = jnp.zeros_like(l_scr)
        acc_scr[...] = jnp.zeros_like(acc_scr)

    def sweep(g, k, v, sx, causal):
        mask = jnp.concatenate([sx] * HPG, axis=0) > 0.5
        if causal is not None:
            mask = mask & causal
        state = (m_scr[g], l_scr[g], acc_scr[g])
        m_scr[g], l_scr[g], acc_scr[g] = _flash_tile(q4s[g], k, v, mask, state)

    for g in range(G_NSA):
        for i, pr in enumerate(page_refs):
            dst = slice(i * page, (i + 1) * page)
            kbuf[g, dst, :] = pr[0, pl.ds(g, page, stride=slots), :].astype(BF16)
            vbuf[g, dst, :] = pr[0, pl.ds(G_NSA + g, page, stride=slots), :].astype(BF16)
        sweep(g, kbuf[g], vbuf[g], selx_ref[0, g], None)

    @pl.when(jb == pl.num_programs(1) - 1)
    def _():
        nrow = new_ref.shape[1]
        lane_n = lax.broadcasted_iota(jnp.int32, (rows, nrow), 1)
        for g in range(G_NSA):
            kcol, vcol = g * HEAD_DIM, (G_NSA + g) * HEAD_DIM
            sweep(g, new_ref[0, :, kcol:kcol + HEAD_DIM].astype(BF16),
                  new_ref[0, :, vcol:vcol + HEAD_DIM].astype(BF16),
                  selx_new_ref[0, g, :, 0:nrow], ((past + lane_n) <= qpos4) & (lane_n < nq))
        gates = gate_ref[0]
        for g in range(G_NSA):
            o_slc = _flash_out((m_scr[g], l_scr[g], acc_scr[g]))
            for n in range(HPG):
                h = g * HPG + n
                cols = slice(h * HEAD_DIM, (h + 1) * HEAD_DIM)
                o_ref[0, :, cols] = (gates[:, 3 * h:3 * h + 1] * ocmp_ref[0, :, cols]
                                     + gates[:, 3 * h + 1:3 * h + 2] * o_slc[n * nq:(n + 1) * nq]
                                     + gates[:, 3 * h + 2:3 * h + 3] * owin_ref[0, :, cols])


def nsa_sample_slc(cache_slc, page_table, q_nsa, slc_new, selx, o_cmp, o_win, gates):
    n_pool, page = cache_slc.shape[:2]
    bsz, n_pages = page_table.shape
    nq = q_nsa.shape[1]
    past = n_pages * page
    pages = min(PAGES_PER_STEP_NSA, n_pages)
    prow = page * 2 * G_NSA
    cache = cache_slc.reshape(n_pool, prow, HEAD_DIM)
    rows = HPG * nq

    def page_map(i):
        return lambda b, jb, pt: (pt[b, jb * pages + i], 0, 0)

    per_b = lambda b, jb, pt: (b, 0, 0)
    grid_spec = pltpu.PrefetchScalarGridSpec(
        num_scalar_prefetch=1,
        grid=(bsz, n_pages // pages),
        in_specs=[pl.BlockSpec((1, prow, HEAD_DIM), page_map(i)) for i in range(pages)]
        + [pl.BlockSpec((1, nq, NSA_W), per_b),
           pl.BlockSpec((1,) + slc_new.shape[1:], per_b),
           pl.BlockSpec((1, G_NSA, nq, pages * page), lambda b, jb, pt: (b, 0, 0, jb)),
           pl.BlockSpec((1, G_NSA, nq, LANES), lambda b, jb, pt: (b, 0, 0, past // LANES)),
           pl.BlockSpec((1, nq, NSA_W), per_b),
           pl.BlockSpec((1, nq, NSA_W), per_b),
           pl.BlockSpec((1, nq, GATE_W), per_b)],
        out_specs=pl.BlockSpec((1, nq, NSA_W), per_b),
        scratch_shapes=[pltpu.VMEM((G_NSA, pages * page, HEAD_DIM), BF16),
                        pltpu.VMEM((G_NSA, pages * page, HEAD_DIM), BF16),
                        pltpu.VMEM((G_NSA, rows, 1), F32), pltpu.VMEM((G_NSA, rows, 1), F32),
                        pltpu.VMEM((G_NSA, rows, HEAD_DIM), F32)],
    )
    return pl.pallas_call(
        functools.partial(_nsa_sample_slc_kernel, pages=pages, nq=nq, past=past),
        out_shape=jax.ShapeDtypeStruct((bsz, nq, NSA_W), F32),
        grid_spec=grid_spec,
        compiler_params=_cparams(("arbitrary", "arbitrary")),
        name="nsa_sample_slc",
    )(page_table, *([cache] * pages), q_nsa, slc_new, selx, selx, o_cmp, o_win, gates)


def _outproj_kernel(osb_ref, onsa_ref, gsb_ref, gnsa_ref, wa_ref, wb_ref, x_ref, gate_ref,
                    g2_ref, shift_ref, scale_ref, x1_ref, h2_ref):
    a = _rms(osb_ref[...], gsb_ref[...]).astype(BF16)
    b = _rms(onsa_ref[...], gnsa_ref[...]).astype(BF16)
    mixed = _dot(a, wa_ref[...]) + _dot(b, wb_ref[...])
    x1 = x_ref[...] + gate_ref[0] * mixed
    x1_ref[...] = x1
    h2_ref[...] = (_rms(x1, g2_ref[...]) * (1.0 + scale_ref[0]) + shift_ref[0]).astype(BF16)


def out_projection(o_sb, o_nsa, g_sb, g_nsa, w_out_bf, x2d, gate, g2, shift, scale, tm, mod_map):
    r, d = x2d.shape
    mod_block = (1,) + gate.shape[1:]
    mod_spec = pl.BlockSpec(mod_block, lambda i: (mod_map(i), 0, 0))
    row = lambda w: pl.BlockSpec((tm, w), lambda i: (i, 0))
    const = lambda shp: pl.BlockSpec(shp, lambda i: (0,) * len(shp))
    return pl.pallas_call(
        _outproj_kernel,
        out_shape=[jax.ShapeDtypeStruct((r, d), F32), jax.ShapeDtypeStruct((r, d), BF16)],
        grid=(r // tm,),
        in_specs=[row(SB_W), row(NSA_W), const((1, SB_W)), const((1, NSA_W)),
                  pl.BlockSpec((SB_W, d), lambda i: (0, 0)),
                  pl.BlockSpec((NSA_W, d), lambda i: (1, 0)),
                  row(d), mod_spec, const((1, d)), mod_spec, mod_spec],
        out_specs=[row(d), row(d)],
        compiler_params=_cparams(("arbitrary",)),
        name="out_projection",
    )(o_sb, o_nsa, g_sb, g_nsa, w_out_bf, w_out_bf, x2d, gate, g2, shift, scale)


def _peer_score_kernel(h_ref, wq_ref, k1_ref, k2_ref, s_ref):
    q = _dot(h_ref[...], wq_ref[...]).astype(BF16)
    half = D_QUERY // 2
    nchunk = h_ref.shape[0] // LANES
    for h in range(PEER_HEADS):
        for c, k_ref in enumerate((k1_ref, k2_ref)):
            qh = q[:, h * D_QUERY + c * half:h * D_QUERY + (c + 1) * half]
            st = _dot_nt(k_ref[h], qh)
            for ch in range(nchunk):
                s_ref[c, h, ch] = st[:, ch * LANES:(ch + 1) * LANES]


def peer_scores(h2, wq_bf, k1_bf, k2_bf):
    r, d = h2.shape
    tm = PEER_TN
    nchunk = tm // LANES
    return pl.pallas_call(
        _peer_score_kernel,
        out_shape=jax.ShapeDtypeStruct((2, PEER_HEADS, r // LANES, N_KEYS, LANES), F32),
        grid=(r // tm,),
        in_specs=[pl.BlockSpec((tm, d), lambda i: (i, 0)),
                  pl.BlockSpec(wq_bf.shape, lambda i: (0, 0)),
                  pl.BlockSpec(k1_bf.shape, lambda i: (0, 0, 0)),
                  pl.BlockSpec(k2_bf.shape, lambda i: (0, 0, 0))],
        out_specs=pl.BlockSpec((2, PEER_HEADS, nchunk, N_KEYS, LANES), lambda i: (0, 0, i, 0, 0)),
        compiler_params=_cparams(("arbitrary",)),
        name="peer_scores",
    )(h2, wq_bf, k1_bf, k2_bf)


def _top_values(x, n, scr):
    for it in range(n):
        m = jnp.max(x, axis=0, keepdims=True)
        scr[it:it + 1, :] = m
        x = jnp.where(x == m, NEG_INF, x)
    return scr[0:n, :]


def _peer_select_kernel(s_ref, s1_ref, s2_ref, tau_ref, v1_scr, v2_scr):
    k = PEER_TOPK
    nchunk = s_ref.shape[2]

    def body(idx, _):
        h, ch = idx // nchunk, idx % nchunk
        s1 = s_ref[0, h, ch]
        s2 = s_ref[1, h, ch]
        v1 = _top_values(s1, k, v1_scr)
        v2 = _top_values(s2, k, v2_scr)
        t1, t2 = v1[k - 1:k], v2[k - 1:k]
        def cands(a1):
            out = [a1[0:1] + v2]
            out += [a1[i:i + 1] + v2[0:k // 2] for i in range(1, k // 2)]
            out += [a1[k // 2:k] + v2[0:1]]
            return jnp.concatenate(out, axis=0)
        x = cands(v1)
        xs = x
        for _ in range(k - 1):
            xs = jnp.where(xs == jnp.max(xs, axis=0, keepdims=True), NEG_INF, xs)
        tau = jnp.max(xs, axis=0, keepdims=True)
        vmax = v1[0:1] + v2[0:1]
        chosen = x >= tau
        z = jnp.sum(jnp.where(chosen, jnp.exp(x - vmax), 0.0), axis=0, keepdims=True)
        shift = vmax + jnp.log(z)
        v2 = v2 * LOG2E
        y = cands((v1 - shift) * LOG2E)
        tau_y = jnp.min(jnp.where(chosen, y, -NEG_INF), axis=0, keepdims=True)
        s1_ref[h, ch] = jnp.where(s1 >= t1, (s1 - shift) * LOG2E, NEG_INF)
        s2_ref[h, ch] = jnp.where(s2 >= t2, s2 * LOG2E, NEG_INF)
        tau_ref[h, ch] = jnp.broadcast_to(tau_y, (SUBLANES, LANES))
        return 0

    lax.fori_loop(0, PEER_HEADS * nchunk, body, 0)


def peer_select(s):
    _, heads, nch_all, keys, lanes = s.shape
    nchunk = PEER_TN // LANES
    blk = (heads, nchunk, keys, lanes)
    return pl.pallas_call(
        _peer_select_kernel,
        out_shape=[jax.ShapeDtypeStruct((heads, nch_all, keys, lanes), F32),
                   jax.ShapeDtypeStruct((heads, nch_all, keys, lanes), F32),
                   jax.ShapeDtypeStruct((heads, nch_all, SUBLANES, lanes), F32)],
        grid=(nch_all // nchunk,),
        in_specs=[pl.BlockSpec((2,) + blk, lambda i: (0, 0, i, 0, 0))],
        out_specs=[pl.BlockSpec(blk, lambda i: (0, i, 0, 0)),
                   pl.BlockSpec(blk, lambda i: (0, i, 0, 0)),
                   pl.BlockSpec((heads, nchunk, SUBLANES, lanes), lambda i: (0, i, 0, 0))],
        scratch_shapes=[pltpu.VMEM((PEER_TOPK, LANES), F32), pltpu.VMEM((PEER_TOPK, LANES), F32)],
        compiler_params=_cparams(("arbitrary",)),
        name="peer_select",
    )(s)


def _peer_expert_kernel(h_ref, u_ref, vt_ref, s1_ref, s2_ref, tau_ref, o_ref, ga0, ga1, g0, g1):
    e = pl.program_id(1)
    ne = pl.num_programs(1) - 2
    nchunk = h_ref.shape[0] // LANES

    @pl.when(e == 0)
    def _():
        o_ref[...] = jnp.zeros_like(o_ref)
        ga0[...] = jnp.zeros_like(ga0)
        g1[...] = jnp.zeros_like(g1)

    def step(g_cur, g_prev, ga_prev, ga_prev2):
        et = jnp.minimum(e, ne - 1)
        for ch in range(nchunk):
            for al in range(PEER_AT):
                a = et * PEER_AT + al
                gsum = jnp.zeros((N_KEYS, LANES), F32)
                for h in range(PEER_HEADS):
                    y = s1_ref[h, ch, pl.ds(a, 1), :] + s2_ref[h, ch]
                    gsum = gsum + jnp.where(y >= tau_ref[h, ch, 0:1, :], jnp.exp2(y), 0.0)
                g_cur[al * N_KEYS:(al + 1) * N_KEYS, ch * LANES:(ch + 1) * LANES] = gsum
        o_ref[...] += _dot(vt_ref[...], ga_prev2[...])
        act = _gelu(_dot_nt(u_ref[...], h_ref[...]))
        ga_prev[...] = (act * g_prev[...]).astype(BF16)

    @pl.when(e % 2 == 0)
    def _():
        step(g0, g1, ga1, ga0)

    @pl.when(e % 2 == 1)
    def _():
        step(g1, g0, ga0, ga1)


def peer_experts(h2, u_bf, vt_bf, s1m, s2m, tau):
    r, d = h2.shape
    n_exp = u_bf.shape[0]
    tn = PEER_TN
    nchunk = tn // LANES
    et = PEER_AT * N_KEYS
    sblk = (PEER_HEADS, nchunk, N_KEYS, LANES)
    ne = n_exp // et
    return pl.pallas_call(
        _peer_expert_kernel,
        out_shape=jax.ShapeDtypeStruct((d, r), F32),
        grid=(r // tn, ne + 2),
        in_specs=[pl.BlockSpec((tn, d), lambda t, e: (t, 0)),
                  pl.BlockSpec((et, d), lambda t, e: (jnp.clip(e - 1, 0, ne - 1), 0)),
                  pl.BlockSpec((d, et), lambda t, e: (0, jnp.clip(e - 2, 0, ne - 1))),
                  pl.BlockSpec(sblk, lambda t, e: (0, t, 0, 0)),
                  pl.BlockSpec(sblk, lambda t, e: (0, t, 0, 0)),
                  pl.BlockSpec((PEER_HEADS, nchunk, SUBLANES, LANES), lambda t, e: (0, t, 0, 0))],
        out_specs=pl.BlockSpec((d, tn), lambda t, e: (0, t)),
        scratch_shapes=[pltpu.VMEM((et, tn), BF16), pltpu.VMEM((et, tn), BF16),
                        pltpu.VMEM((et, tn), F32), pltpu.VMEM((et, tn), F32)],
        compiler_params=_cparams(("arbitrary", "arbitrary")),
        name="peer_experts",
    )(h2, u_bf, vt_bf, s1m, s2m, tau)


def _final_kernel(x1_ref, pt_ref, gate_ref, g_ref, y_ref):
    x2 = x1_ref[...] + gate_ref[0] * pt_ref[...].T
    y_ref[...] = _rms(x2, g_ref[...])


def final_norm(x1, peer_t, col0, gate, g, tm, mod_map):
    r, d = x1.shape
    mod_block = (1,) + gate.shape[1:]
    return pl.pallas_call(
        _final_kernel,
        out_shape=jax.ShapeDtypeStruct((r, d), F32),
        grid=(r // tm,),
        in_specs=[pl.BlockSpec((tm, d), lambda i: (i, 0)),
                  pl.BlockSpec((d, tm), lambda i: (0, col0 // tm + i)),
                  pl.BlockSpec(mod_block, lambda i: (mod_map(i), 0, 0)),
                  pl.BlockSpec((1, d), lambda i: (0, 0))],
        out_specs=pl.BlockSpec((tm, d), lambda i: (i, 0)),
        compiler_params=_cparams(("arbitrary",)),
        name="final_norm",
    )(x1, peer_t, gate, g)


def _layer(x_prompt, x_sample, cache_sb, cache_cmp, cache_slc, state_win, page_table, c_prompt,
           c_sample, w_ada, b_ada, norm1_g, w_in, cmp_pe_k, cmp_w1_k, cmp_w2_k, cmp_pe_v, cmp_w1_v,
           cmp_w2_v, out_g_sb, out_g_nsa, w_out, norm2_g, peer_wq, peer_k1, peer_k2, peer_u, peer_v):
    b, t, d = x_prompt.shape
    bs, nq, _ = x_sample.shape
    n_pages, page = page_table.shape[1], cache_sb.shape[1]
    past = n_pages * page
    rp, rs = b * t, bs * nq

    w_in_bf = jnp.pad(w_in, ((0, 0), (0, IN_NT * IN_TN - D_IN))).astype(BF16)
    w_out_bf = w_out.astype(BF16)
    wq_bf = peer_wq.astype(BF16)
    k1_bf, k2_bf = peer_k1.astype(BF16), peer_k2.astype(BF16)
    u_bf = peer_u.astype(BF16)
    vt_bf = peer_v.astype(BF16).T
    pe = jnp.stack([cmp_pe_k, cmp_pe_v])
    w1 = jnp.stack([cmp_w1_k, cmp_w1_v]).astype(BF16).reshape(2, CMP_BLK * HEAD_DIM, HEAD_DIM)
    w2 = jnp.stack([cmp_w2_k, cmp_w2_v]).astype(BF16)
    row = lambda v: v.reshape(1, -1)

    n_c = b + bs
    c_all = jnp.pad(jnp.concatenate([c_prompt, c_sample]), ((0, -n_c % 16), (0, 0))).astype(BF16)
    mod = adaln_mod(c_all, w_ada, row(b_ada))
    mod_p = [mod[:b, k * d:(k + 1) * d].reshape(b, 1, d) for k in range(6)]
    mod_s = [jnp.repeat(mod[b:n_c, k * d:(k + 1) * d], nq, axis=0).reshape(1, rs, d) for k in range(6)]

    tm_p = 512 if t % 512 == 0 else t
    xp = x_prompt.reshape(rp, d)
    tabs_p = _rope_tables(jnp.arange(t))
    qsb_p, kvsb_p, qnsa_p, kvcmp_p, kvslc_p, kvwin_p, gates_p = in_projection(
        xp, row(norm1_g), mod_p[0], mod_p[1], w_in_bf, tabs_p, tm_p,
        lambda i: i // (t // tm_p), lambda i: i % (t // tm_p))
    osb_p = sb_prompt(qsb_p, kvsb_p, b, t)
    kvc_p = compress_prompt(kvcmp_p, pe, w1, w2, b, t)
    onsa_p = nsa_prompt(qnsa_p, kvc_p, kvslc_p, kvwin_p, gates_p, b, t)
    tm_o = 256
    x1_p, h2_p = out_projection(osb_p, onsa_p, row(out_g_sb), row(out_g_nsa), w_out_bf, xp,
                                mod_p[2], row(norm2_g), mod_p[3], mod_p[4], tm_o,
                                lambda i: i // (t // tm_o))

    xs = x_sample.reshape(rs, d)
    tabs_s = _rope_tables(past + jnp.tile(jnp.arange(nq), bs))
    qsb_s, kvsb_s, qnsa_s, kvcmp_s, kvslc_s, kvwin_s, gates_s = in_projection(
        xs, row(norm1_g), mod_s[0], mod_s[1], w_in_bf, tabs_s, rs, lambda i: 0, lambda i: 0)
    osb_s = sb_sample(cache_sb, page_table, qsb_s.reshape(bs, nq, SB_W), kvsb_s.reshape(bs, nq, 2 * SB_W))
    kvcmp_s3 = kvcmp_s.reshape(bs, nq, KV_W)
    kvc_s = compress_sample(cache_cmp, page_table, kvcmp_s3, pe, w1, w2)
    pad8 = lambda a: jnp.pad(a.reshape(bs, nq, KV_W), ((0, 0), (0, -nq % SUBLANES), (0, 0)))
    qnsa_s3 = qnsa_s.reshape(bs, nq, NSA_W)
    st_win = state_win.reshape(bs, state_win.shape[1] * 2 * G_NSA, HEAD_DIM)
    ocmp_s, owin_s, selx = nsa_sample_select(qnsa_s3, kvc_s, st_win, pad8(kvwin_s), past)
    onsa_s = nsa_sample_slc(cache_slc, page_table, qnsa_s3, pad8(kvslc_s), selx, ocmp_s, owin_s,
                            gates_s.reshape(bs, nq, GATE_W))
    x1_s, h2_s = out_projection(osb_s.reshape(rs, SB_W), onsa_s.reshape(rs, NSA_W), row(out_g_sb),
                                row(out_g_nsa), w_out_bf, xs, mod_s[2], row(norm2_g), mod_s[3],
                                mod_s[4], rs, lambda i: 0)

    r = rp + rs
    r_pad = -(-r // PEER_TN) * PEER_TN
    h2 = jnp.pad(jnp.concatenate([h2_p, h2_s]), ((0, r_pad - r), (0, 0)))
    s = peer_scores(h2, wq_bf, k1_bf, k2_bf)
    s1m, s2m, tau = peer_select(s)
    peer_t = peer_experts(h2, u_bf, vt_bf, s1m, s2m, tau)

    states_p = (kvsb_p.reshape(b, t, 2, H_SB, HEAD_DIM), kvcmp_p.reshape(b, t, 2, G_NSA, HEAD_DIM),
                kvslc_p.reshape(b, t, 2, G_NSA, HEAD_DIM),
                kvwin_p.reshape(b, t, 2, G_NSA, HEAD_DIM)[:, t - min(WINDOW, t):])
    kvwin_s5 = kvwin_s.reshape(bs, nq, 2, G_NSA, HEAD_DIM)
    states_s = (kvsb_s.reshape(bs, nq, 2, H_SB, HEAD_DIM), kvcmp_s.reshape(bs, nq, 2, G_NSA, HEAD_DIM),
                kvslc_s.reshape(bs, nq, 2, G_NSA, HEAD_DIM),
                jnp.concatenate([state_win, kvwin_s5], axis=1)[:, nq:])
    return (x1_p, mod_p[5], x1_s, mod_s[5], peer_t, rp), states_p, states_s


def kernel(x_prompt, x_sample, cache_sb, cache_cmp, cache_slc, state_win, page_table, c_prompt, c_sample, w_ada, b_ada, norm1_g, w_in, cmp_pe_k, cmp_w1_k, cmp_w2_k, cmp_pe_v, cmp_w1_v, cmp_w2_v, out_g_sb, out_g_nsa, w_out, norm2_g, peer_wq, peer_k1, peer_k2, peer_u, peer_v, final_g):
    depth = w_ada.shape[0]
    assert depth == 1, "single-layer trunk"
    b, t, d = x_prompt.shape
    bs, nq, _ = x_sample.shape
    (x1_p, gate_p, x1_s, gate_s, peer_t, rp), st_p, st_s = _layer(
        x_prompt, x_sample, cache_sb[0], cache_cmp[0], cache_slc[0], state_win[0], page_table,
        c_prompt, c_sample, w_ada[0], b_ada[0], norm1_g[0], w_in[0], cmp_pe_k[0], cmp_w1_k[0],
        cmp_w2_k[0], cmp_pe_v[0], cmp_w1_v[0], cmp_w2_v[0], out_g_sb[0], out_g_nsa[0], w_out[0],
        norm2_g[0], peer_wq[0], peer_k1[0], peer_k2[0], peer_u[0], peer_v[0])
    fg = final_g.reshape(1, d)
    tm_f = 256
    y_p = final_norm(x1_p, peer_t, 0, gate_p, fg, tm_f, lambda i: i // (t // tm_f)).reshape(b, t, d)
    rs = bs * nq
    y_s = final_norm(x1_s, peer_t, rp, gate_s, fg, rs, lambda i: 0).reshape(bs, nq, d)
    return (y_p, y_s, st_p[0][None], st_s[0][None], st_p[1][None], st_s[1][None],
            st_p[2][None], st_s[2][None], st_p[3][None], st_s[3][None])
```

```python
import functools

import jax
import jax.numpy as jnp
from jax import lax
from jax.experimental import pallas as pl
from jax.experimental.pallas import tpu as pltpu

F32 = jnp.float32
BF16 = jnp.bfloat16

HEAD_DIM = 128
H_SB = 8
H_NSA = 8
G_NSA = 2
HPG = H_NSA // G_NSA
SB_W = H_SB * HEAD_DIM
NSA_W = H_NSA * HEAD_DIM
KV_W = 2 * G_NSA * HEAD_DIM
D_IN = 3 * SB_W + NSA_W + 3 * KV_W + 3 * H_NSA
SCALE = HEAD_DIM ** -0.5
ROT_HALF = HEAD_DIM // 8
ROPE_THETA = 500000.0
CMP_BLK = 64
TOP_BLOCKS = 16
WINDOW = 512
FORCE_SCORE = 1e9
N_KEYS = 128
PEER_HEADS = 8
PEER_TOPK = 16
D_QUERY = 256
EPS = 1e-6
NEG_INF = -1e30

LANES = 128
SUBLANES = 8
VMEM_LIMIT = 56 * 1024 * 1024

MXU_N = 256
IN_TN = 2 * MXU_N
IN_NT = -(-D_IN // IN_TN)
GATE_W = LANES
PEER_TN = 3 * MXU_N
CMP_PITCH = CMP_BLK * 2 * G_NSA + SUBLANES
PEER_AT = 4
PAGES_PER_STEP_SB = 8
PAGES_PER_STEP_NSA = 16


def _cparams(sem):
    return pltpu.CompilerParams(dimension_semantics=sem, vmem_limit_bytes=VMEM_LIMIT)


def _dot(a, b):
    return jnp.dot(a, b, preferred_element_type=F32)


def _dot_nt(a, b):
    return lax.dot_general(a, b, (((1,), (1,)), ((), ())), preferred_element_type=F32)


def _dot_tn(a, b):
    return lax.dot_general(a, b, (((0,), (0,)), ((), ())), preferred_element_type=F32)


def _gelu(x):
    return 0.5 * x * (1.0 + jnp.tanh(0.7978845608028654 * (x + 0.044715 * (x * x * x))))


def _rms(x, g):
    return x * lax.rsqrt(jnp.mean(x * x, axis=-1, keepdims=True) + EPS) * g


def _softplus(z):
    return jnp.maximum(z, 0.0) + jnp.log(1.0 + jnp.exp(-jnp.abs(z)))


def _split_bf16(x):
    hi = x.astype(BF16)
    lo = (x - hi.astype(F32)).astype(BF16)
    return hi, lo


def _mod_kernel(c_ref, w_ref, b_ref, o_ref):
    o_ref[...] = _dot(c_ref[...], w_ref[...].astype(BF16)) + b_ref[...]


def adaln_mod(c_bf, w_ada, b_ada):
    m, d = c_bf.shape
    n = w_ada.shape[1]
    tn = 1024
    return pl.pallas_call(
        _mod_kernel,
        out_shape=jax.ShapeDtypeStruct((m, n), F32),
        grid=(n // tn,),
        in_specs=[pl.BlockSpec((m, d), lambda j: (0, 0)),
                  pl.BlockSpec((d, tn), lambda j: (0, j)),
                  pl.BlockSpec((1, tn), lambda j: (0, j))],
        out_specs=pl.BlockSpec((m, tn), lambda j: (0, j)),
        compiler_params=_cparams(("arbitrary",)),
        name="adaln_mod",
    )(c_bf, w_ada, b_ada)


_IN_SEGMENTS = (
    (0, 2, 0, "plain"),
    (2, 6, 1, "plain"),
    (6, 8, 2, "rope"),
    (8, 9, 3, "rope_k"),
    (9, 10, 4, "rope_k"),
    (10, 11, 5, "rope_k"),
    (11, 12, 6, "gate"),
)
_IN_OUT_RANGES = ((0, 2), (2, 6), (6, 8), (8, 9), (9, 10), (10, 11), (11, 12))


def _rope(p, rc, ra, rb):
    w = p.shape[1]
    return p * rc + pltpu.roll(p, ROT_HALF, 1) * ra + pltpu.roll(p, w - ROT_HALF, 1) * rb


def _inproj_kernel(x_ref, g_ref, shift_ref, scale_ref, w_ref, rc_ref, ra_ref, rb_ref,
                   *rest):
    outs, h_scr = rest[:7], rest[7]
    j = pl.program_id(1)

    @pl.when(j == 0)
    def _():
        h = _rms(x_ref[...], g_ref[...]) * (1.0 + scale_ref[0]) + shift_ref[0]
        h_scr[...] = h.astype(BF16)

    p = _dot(h_scr[...], w_ref[...])
    kw = G_NSA * HEAD_DIM
    for lo, hi, slot, kind in _IN_SEGMENTS:
        @pl.when((j >= lo) & (j < hi))
        def _(slot=slot, kind=kind):
            if kind == "rope":
                v = _rope(p, rc_ref[...], ra_ref[...], rb_ref[...])
            elif kind == "rope_k":
                v = jnp.concatenate([_rope(p[:, :kw], rc_ref[:, :kw], ra_ref[:, :kw], rb_ref[:, :kw]),
                                     p[:, kw:]], axis=1)
            elif kind == "gate":
                v = jax.nn.sigmoid(p[:, :GATE_W])
            else:
                v = p
            outs[slot][...] = v.astype(outs[slot].dtype)


def _rope_tables(pos):
    inv = jnp.power(ROPE_THETA, -jnp.arange(ROT_HALF, dtype=F32) / ROT_HALF)
    ang = pos.astype(F32)[:, None] * inv[None, :]
    cos, sin = jnp.cos(ang), jnp.sin(ang)
    n = pos.shape[0]
    z16 = jnp.zeros((n, ROT_HALF), F32)
    tail0 = jnp.zeros((n, HEAD_DIM - 2 * ROT_HALF), F32)
    rc = jnp.concatenate([cos, cos, tail0 + 1.0], axis=1)
    ra = jnp.concatenate([z16, sin, tail0], axis=1)
    rb = jnp.concatenate([-sin, z16, tail0], axis=1)
    heads = IN_TN // HEAD_DIM
    return tuple(jnp.tile(t, (1, heads)) for t in (rc, ra, rb))


def in_projection(x2d, g1, shift, scale, w_bf, tables, tm, mod_map, tab_map):
    r, d = x2d.shape
    widths = [(hi - lo) * IN_TN for lo, hi in _IN_OUT_RANGES[:-1]] + [GATE_W]
    dtypes = [BF16, F32, BF16, F32, F32, F32, F32]
    out_shape = [jax.ShapeDtypeStruct((r, w), dt) for w, dt in zip(widths, dtypes)]
    blocks = [IN_TN] * 6 + [GATE_W]

    def out_map(lo, hi):
        return lambda i, j: (i, jnp.clip(j - lo, 0, hi - lo - 1))

    mod_block = (1,) + shift.shape[1:]
    return pl.pallas_call(
        _inproj_kernel,
        out_shape=out_shape,
        grid=(r // tm, IN_NT),
        in_specs=[pl.BlockSpec((tm, d), lambda i, j: (i, 0)),
                  pl.BlockSpec((1, d), lambda i, j: (0, 0)),
                  pl.BlockSpec(mod_block, lambda i, j: (mod_map(i), 0, 0)),
                  pl.BlockSpec(mod_block, lambda i, j: (mod_map(i), 0, 0)),
                  pl.BlockSpec((d, IN_TN), lambda i, j: (0, j)),
                  pl.BlockSpec((tm, IN_TN), lambda i, j: (tab_map(i), 0)),
                  pl.BlockSpec((tm, IN_TN), lambda i, j: (tab_map(i), 0)),
                  pl.BlockSpec((tm, IN_TN), lambda i, j: (tab_map(i), 0))],
        out_specs=[pl.BlockSpec((tm, w), out_map(lo, hi))
                   for w, (lo, hi) in zip(blocks, _IN_OUT_RANGES)],
        scratch_shapes=[pltpu.VMEM((tm, d), BF16)],
        compiler_params=_cparams(("arbitrary", "arbitrary")),
        name="in_projection",
    )(x2d, g1, shift, scale, w_bf, *tables)


def _sb_weights(z, valid, later, tri):
    rows, keys = z.shape
    sub = tri.shape[0]
    nsub = keys // sub
    sp = _softplus(z)
    log1m = -sp if valid is None else jnp.where(valid, -sp, 0.0)
    stacked = jnp.concatenate([log1m[:, i * sub:(i + 1) * sub] for i in range(nsub)], axis=0)
    hi, lo = _split_bf16(stacked)
    within = _dot(hi, tri) + _dot(lo, tri)
    afters = [None] * nsub
    for i in reversed(range(nsub)):
        w = within[i * rows:(i + 1) * rows]
        afters[i] = w + later
        later = later + w[:, 0:1] + log1m[:, i * sub:i * sub + 1]
    a = jnp.exp(z - sp + jnp.concatenate(afters, axis=1))
    if valid is not None:
        a = jnp.where(valid, a, 0.0)
    return a, later


def _later_keys_tri(sub):
    return (lax.broadcasted_iota(jnp.int32, (sub, sub), 0)
            > lax.broadcasted_iota(jnp.int32, (sub, sub), 1)).astype(BF16)


def _sb_prompt_kernel(q_ref, k_ref, v_ref, o_ref, *, tq, tk):
    qi = pl.program_id(2)
    q = q_ref[...]
    qpos = qi * tq + lax.broadcasted_iota(jnp.int32, (tq, tk), 0)
    lane = lax.broadcasted_iota(jnp.int32, (tq, tk), 1)
    tri = _later_keys_tri(LANES)
    nk = ((qi + 1) * tq + tk - 1) // tk

    def body(it, carry):
        later, acc = carry
        start = pl.multiple_of((nk - 1 - it) * tk, tk)
        k = k_ref[pl.ds(start, tk), :].astype(BF16)
        v = v_ref[pl.ds(start, tk), :].astype(BF16)
        z = _dot_nt(q, k) * SCALE
        a, later = _sb_weights(z, (start + lane) < qpos, later, tri)
        return later, acc + _dot(a.astype(BF16), v)

    _, acc = lax.fori_loop(0, nk, body, (jnp.zeros((tq, 1), F32), jnp.zeros((tq, HEAD_DIM), F32)))
    o_ref[...] = acc


def sb_prompt(q_sb, kv_sb, b, t):
    tq = 256
    tk = 512 if t % 512 == 0 else tq
    nq = t // tq
    return pl.pallas_call(
        functools.partial(_sb_prompt_kernel, tq=tq, tk=tk),
        out_shape=jax.ShapeDtypeStruct((b * t, SB_W), F32),
        grid=(b, H_SB, nq),
        in_specs=[pl.BlockSpec((tq, HEAD_DIM), lambda bi, h, qi: (bi * nq + qi, h)),
                  pl.BlockSpec((t, HEAD_DIM), lambda bi, h, qi: (bi, h)),
                  pl.BlockSpec((t, HEAD_DIM), lambda bi, h, qi: (bi, H_SB + h))],
        out_specs=pl.BlockSpec((tq, HEAD_DIM), lambda bi, h, qi: (bi * nq + qi, h)),
        compiler_params=_cparams(("arbitrary", "arbitrary", "arbitrary")),
        name="sb_prompt",
    )(q_sb, kv_sb, kv_sb)


def _compress_rows(read_rows, pe_ref, w1_ref, w2_ref, x_scr, c):
    for p in range(CMP_BLK):
        x = jnp.concatenate([read_rows(p, 0), read_rows(p, 1)], axis=0) + pe_ref[c, p:p + 1, :]
        x_scr[:, p * HEAD_DIM:(p + 1) * HEAD_DIM] = x.astype(BF16)
    hid = _gelu(_dot(x_scr[...], w1_ref[c]))
    return _dot(hid.astype(BF16), w2_ref[c])


def _cmp_prompt_kernel(x_ref, pe_ref, w1_ref, w2_ref, o_ref, x_scr, *, nb):
    stride = CMP_BLK * 2 * G_NSA
    for c in range(2):
        def read_rows(p, g, c=c):
            return x_ref[0, pl.ds(p * 2 * G_NSA + c * G_NSA + g, nb, stride=stride), :]
        out = _compress_rows(read_rows, pe_ref, w1_ref, w2_ref, x_scr, c)
        o_ref[0, c, 0] = out[:nb]
        o_ref[0, c, 1] = out[nb:]


def compress_prompt(kv_cmp, pe, w1, w2, b, t):
    nb = t // CMP_BLK
    x = kv_cmp.reshape(b, t * 2 * G_NSA, HEAD_DIM)
    return pl.pallas_call(
        functools.partial(_cmp_prompt_kernel, nb=nb),
        out_shape=jax.ShapeDtypeStruct((b, 2, G_NSA, nb, HEAD_DIM), F32),
        grid=(b,),
        in_specs=[pl.BlockSpec((1, t * 2 * G_NSA, HEAD_DIM), lambda i: (i, 0, 0)),
                  pl.BlockSpec(pe.shape, lambda i: (0, 0, 0)),
                  pl.BlockSpec(w1.shape, lambda i: (0, 0, 0)),
                  pl.BlockSpec(w2.shape, lambda i: (0, 0, 0))],
        out_specs=pl.BlockSpec((1, 2, G_NSA, nb, HEAD_DIM), lambda i: (i, 0, 0, 0, 0)),
        scratch_shapes=[pltpu.VMEM((G_NSA * nb, CMP_BLK * HEAD_DIM), BF16)],
        compiler_params=_cparams(("arbitrary",)),
        name="compress_prompt",
    )(x, pe, w1, w2)


def _cmp_sample_kernel(pt_ref, *refs, pages, nbp, past):
    page_refs = refs[:pages]
    new_ref, pe_ref, w1_ref, w2_ref, o_ref, buf, x_scr = refs[pages:]
    b, jb = pl.program_id(0), pl.program_id(1)
    blk_rows = CMP_BLK * 2 * G_NSA
    blks_per_page = page_refs[0].shape[1] // blk_rows
    past_blks = past // CMP_BLK

    @pl.when((b == 0) & (jb == 0))
    def _():
        tail = buf.shape[0] - past_blks * CMP_PITCH
        buf[pl.ds(past_blks * CMP_PITCH, tail), :] = jnp.zeros((tail, HEAD_DIM), F32)

    for i, pr in enumerate(page_refs):
        for hb in range(blks_per_page):
            blk = (jb * pages + i) * blks_per_page + hb
            start = pl.multiple_of(blk * CMP_PITCH, SUBLANES)
            buf[pl.ds(start, blk_rows), :] = pr[0, hb * blk_rows:(hb + 1) * blk_rows, :]

    @pl.when(jb == pl.num_programs(1) - 1)
    def _():
        buf[pl.ds(past_blks * CMP_PITCH, new_ref.shape[1]), :] = new_ref[0]
        for c in range(2):
            def read_rows(p, g, c=c):
                return buf[pl.ds(p * 2 * G_NSA + c * G_NSA + g, nbp, stride=CMP_PITCH), :]
            out = _compress_rows(read_rows, pe_ref, w1_ref, w2_ref, x_scr, c)
            o_ref[0, c, 0] = out[:nbp]
            o_ref[0, c, 1] = out[nbp:]


def compress_sample(cache_cmp, page_table, kv_new, pe, w1, w2):
    n_pool, page = cache_cmp.shape[:2]
    bsz, n_pages = page_table.shape
    nq = kv_new.shape[1]
    past = n_pages * page
    nb = -(-(past + nq) // CMP_BLK)
    nbp = -(-nb // SUBLANES) * SUBLANES
    pages = min(PAGES_PER_STEP_NSA, n_pages)
    rpp = page * 2 * G_NSA
    cache = cache_cmp.reshape(n_pool, rpp, HEAD_DIM)
    new = kv_new.reshape(bsz, nq * 2 * G_NSA, HEAD_DIM)

    def page_map(i):
        return lambda b, jb, pt: (pt[b, jb * pages + i], 0, 0)

    grid_spec = pltpu.PrefetchScalarGridSpec(
        num_scalar_prefetch=1,
        grid=(bsz, n_pages // pages),
        in_specs=[pl.BlockSpec((1, rpp, HEAD_DIM), page_map(i)) for i in range(pages)]
        + [pl.BlockSpec((1,) + new.shape[1:], lambda b, jb, pt: (b, 0, 0)),
           pl.BlockSpec(pe.shape, lambda b, jb, pt: (0, 0, 0)),
           pl.BlockSpec(w1.shape, lambda b, jb, pt: (0, 0, 0)),
           pl.BlockSpec(w2.shape, lambda b, jb, pt: (0, 0, 0))],
        out_specs=pl.BlockSpec((1, 2, G_NSA, nbp, HEAD_DIM), lambda b, jb, pt: (b, 0, 0, 0, 0)),
        scratch_shapes=[pltpu.VMEM((nbp * CMP_PITCH, HEAD_DIM), F32),
                        pltpu.VMEM((G_NSA * nbp, CMP_BLK * HEAD_DIM), BF16)],
    )
    return pl.pallas_call(
        functools.partial(_cmp_sample_kernel, pages=pages, nbp=nbp, past=past),
        out_shape=jax.ShapeDtypeStruct((bsz, 2, G_NSA, nbp, HEAD_DIM), F32),
        grid_spec=grid_spec,
        compiler_params=_cparams(("arbitrary", "arbitrary")),
        name="compress_sample",
    )(page_table, *([cache] * pages), new, pe, w1, w2)


def _cmp_branch(q4, qpos4, kc, vc, nq, nbp):
    blk = lax.broadcasted_iota(jnp.int32, (1, nbp), 1)
    s = _dot_nt(q4, kc.astype(BF16)) * SCALE
    complete = (blk + 1) * CMP_BLK <= qpos4 + 1
    m = jnp.max(jnp.where(complete, s, NEG_INF), axis=1, keepdims=True)
    e = jnp.where(complete, jnp.exp(s - m), 0.0)
    den = jnp.sum(e, axis=1, keepdims=True)
    p = e / jnp.where(den > 0.0, den, 1.0)
    o_cmp = _dot(p.astype(BF16), vc.astype(BF16))
    imp = p[0:nq]
    for n in range(1, HPG):
        imp = imp + p[n * nq:(n + 1) * nq]
    qpos = qpos4[0:nq]
    cur = qpos // CMP_BLK
    started = blk <= cur
    forced = started & ((blk == 0) | (blk >= cur - 1))
    score = jnp.where(forced, FORCE_SCORE, jnp.where(started, imp, -FORCE_SCORE))
    rank = jnp.zeros((nq, nbp), F32)
    for j in range(nbp):
        col = score[:, j:j + 1]
        rank = rank + jnp.where(col > score, 1.0, 0.0) + jnp.where((col == score) & (blk > j), 1.0, 0.0)
    return o_cmp, rank


def _cmp_branch_t(q4, qpos_row, kc, vc, nq, nb):
    blk = lax.broadcasted_iota(jnp.int32, (nb, 1), 0)
    s = _dot_nt(kc.astype(BF16), q4) * SCALE
    complete = (blk + 1) * CMP_BLK <= qpos_row + 1
    m = jnp.max(jnp.where(complete, s, NEG_INF), axis=0, keepdims=True)
    e = jnp.where(complete, jnp.exp(s - m), 0.0)
    den = jnp.sum(e, axis=0, keepdims=True)
    p = e / jnp.where(den > 0.0, den, 1.0)
    o_cmp = _dot_tn(p.astype(BF16), vc.astype(BF16))
    imp = p[:, 0:nq]
    for n in range(1, HPG):
        imp = imp + p[:, n * nq:(n + 1) * nq]
    cur = qpos_row[:, 0:nq] // CMP_BLK
    started = blk <= cur
    forced = started & ((blk == 0) | (blk >= cur - 1))
    score = jnp.where(forced, FORCE_SCORE, jnp.where(started, imp, -FORCE_SCORE))
    rank = jnp.zeros((nb, nq), F32)
    for j in range(nb):
        row = score[j:j + 1, :]
        rank = rank + jnp.where(row > score, 1.0, 0.0) + jnp.where((row == score) & (blk > j), 1.0, 0.0)
    return o_cmp, rank


def _flash_tile(q4, k, v, mask, state):
    m, l, acc = state
    s = _dot_nt(q4, k) * SCALE
    m_new = jnp.maximum(m, jnp.max(jnp.where(mask, s, NEG_INF), axis=1, keepdims=True))
    p = jnp.where(mask, jnp.exp(s - m_new), 0.0)
    alpha = jnp.exp(m - m_new)
    l = alpha * l + jnp.sum(p, axis=1, keepdims=True)
    acc = alpha * acc + _dot(p.astype(BF16), v)
    return m_new, l, acc


def _flash_init(rows):
    return (jnp.full((rows, 1), NEG_INF, F32), jnp.zeros((rows, 1), F32),
            jnp.zeros((rows, HEAD_DIM), F32))


def _flash_out(state):
    _, l, acc = state
    return acc / jnp.where(l > 0.0, l, 1.0)


def _stack_heads(q_ref):
    return jnp.concatenate([q_ref[:, n * HEAD_DIM:(n + 1) * HEAD_DIM] for n in range(HPG)], axis=0)


def _nsa_prompt_kernel(q_ref, kc_ref, vc_ref, sk_ref, sv_ref, wk_ref, wv_ref, gate_ref, o_ref,
                       selx_scr, *, tq, tk, nb, topk):
    g, qi = pl.program_id(1), pl.program_id(2)
    rows = HPG * tq
    t = sk_ref.shape[0]
    q4 = _stack_heads(q_ref)
    t_in = lax.broadcasted_iota(jnp.int32, (tq, 1), 0)
    qpos4 = jnp.concatenate([qi * tq + t_in] * HPG, axis=0)
    qpos_row = qi * tq + lax.broadcasted_iota(jnp.int32, (1, rows), 1) % tq
    o_cmp, rank_t = _cmp_branch_t(q4, qpos_row, kc_ref[0, 0, 0], vc_ref[0, 0, 0], tq, nb)
    sel_t = jnp.where(rank_t < topk, 1.0, 0.0).astype(BF16)
    kb = lax.broadcasted_iota(jnp.int32, (nb, t), 0)
    kl = lax.broadcasted_iota(jnp.int32, (nb, t), 1)
    selx = _dot_tn(sel_t, jnp.where(kl // CMP_BLK == kb, 1.0, 0.0).astype(BF16))
    for j in range(t // tk):
        selx_scr[j] = selx[:, j * tk:(j + 1) * tk]
    lane = lax.broadcasted_iota(jnp.int32, (rows, tk), 1)

    def slc_body(j, state):
        start = pl.multiple_of(j * tk, tk)
        k = sk_ref[pl.ds(start, tk), :].astype(BF16)
        v = sv_ref[pl.ds(start, tk), :].astype(BF16)
        sx = selx_scr[j]
        mask = (jnp.concatenate([sx] * HPG, axis=0) > 0.5) & ((start + lane) <= qpos4)
        return _flash_tile(q4, k, v, mask, state)

    nk = ((qi + 1) * tq + tk - 1) // tk
    o_slc = _flash_out(lax.fori_loop(0, nk, slc_body, _flash_init(rows)))

    wlen = min(WINDOW + tq, t)
    start = pl.multiple_of(jnp.clip(qi * tq - WINDOW, 0, t - wlen), tq)
    d = qpos4 - (start + lax.broadcasted_iota(jnp.int32, (rows, wlen), 1))
    o_win = _flash_out(_flash_tile(q4, wk_ref[pl.ds(start, wlen), :].astype(BF16),
                                   wv_ref[pl.ds(start, wlen), :].astype(BF16),
                                   (d >= 0) & (d < WINDOW), _flash_init(rows)))

    gates = gate_ref[...]

    def gate(n, branch):
        col = gates[:, n * 3 + branch:n * 3 + branch + 1]
        for gg in range(1, G_NSA):
            c = (gg * HPG + n) * 3 + branch
            col = jnp.where(g == gg, gates[:, c:c + 1], col)
        return col

    for n in range(HPG):
        r0 = n * tq
        o_ref[:, n * HEAD_DIM:(n + 1) * HEAD_DIM] = (
            gate(n, 0) * o_cmp[r0:r0 + tq] + gate(n, 1) * o_slc[r0:r0 + tq]
            + gate(n, 2) * o_win[r0:r0 + tq])


def nsa_prompt(q_nsa, kvc, kv_slc, kv_win, gates, b, t):
    tq = 128
    tk = 512 if t % 512 == 0 else tq
    nq = t // tq
    nb = t // CMP_BLK
    topk = min(TOP_BLOCKS, nb)
    gw = HPG * HEAD_DIM
    return pl.pallas_call(
        functools.partial(_nsa_prompt_kernel, tq=tq, tk=tk, nb=nb, topk=topk),
        out_shape=jax.ShapeDtypeStruct((b * t, NSA_W), F32),
        grid=(b, G_NSA, nq),
        in_specs=[pl.BlockSpec((tq, gw), lambda bi, g, qi: (bi * nq + qi, g)),
                  pl.BlockSpec((1, 1, 1, nb, HEAD_DIM), lambda bi, g, qi: (bi, 0, g, 0, 0)),
                  pl.BlockSpec((1, 1, 1, nb, HEAD_DIM), lambda bi, g, qi: (bi, 1, g, 0, 0)),
                  pl.BlockSpec((t, HEAD_DIM), lambda bi, g, qi: (bi, g)),
                  pl.BlockSpec((t, HEAD_DIM), lambda bi, g, qi: (bi, G_NSA + g)),
                  pl.BlockSpec((t, HEAD_DIM), lambda bi, g, qi: (bi, g)),
                  pl.BlockSpec((t, HEAD_DIM), lambda bi, g, qi: (bi, G_NSA + g)),
                  pl.BlockSpec((tq, GATE_W), lambda bi, g, qi: (bi * nq + qi, 0))],
        out_specs=pl.BlockSpec((tq, gw), lambda bi, g, qi: (bi * nq + qi, g)),
        scratch_shapes=[pltpu.VMEM((t // tk, tq, tk), F32)],
        compiler_params=_cparams(("arbitrary", "arbitrary", "arbitrary")),
        name="nsa_prompt",
    )(q_nsa, kvc, kvc, kv_slc, kv_slc, kv_win, kv_win, gates)


def _sb_sample_kernel(pt_ref, *refs, pages, nq):
    page_refs = refs[:pages]
    qt_ref, new_ref, o_ref, kbuf, vbuf, acc_scr, later_scr = refs[pages:]
    jb = pl.program_id(1)
    page = new_ref.shape[1]
    rows = H_SB * nq
    tri = _later_keys_tri(LANES)
    qt = qt_ref[0]

    def sweep(k, v, valid):
        z = _dot_nt(qt, k) * SCALE
        a, later = _sb_weights(z, valid, later_scr[...], tri)
        acc_scr[...] += _dot(a.astype(BF16), v)
        later_scr[...] = later

    @pl.when(jb == 0)
    def _():
        acc_scr[...] = jnp.zeros_like(acc_scr)
        later_scr[...] = jnp.zeros_like(later_scr)
        key = lax.broadcasted_iota(jnp.int32, (rows, page), 1)
        qry = lax.broadcasted_iota(jnp.int32, (rows, page), 0) % nq
        sweep(new_ref[0, :, :SB_W].astype(BF16), new_ref[0, :, SB_W:].astype(BF16), key < qry)

    stride = 2 * H_SB
    for i, pr in enumerate(page_refs):
        for h in range(H_SB):
            dst = (slice(i * page, (i + 1) * page), slice(h * HEAD_DIM, (h + 1) * HEAD_DIM))
            kbuf[dst] = pr[0, pl.ds(h, page, stride=stride), :].astype(BF16)
            vbuf[dst] = pr[0, pl.ds(H_SB + h, page, stride=stride), :].astype(BF16)
    sweep(kbuf[...], vbuf[...], None)

    @pl.when(jb == pl.num_programs(1) - 1)
    def _():
        acc = acc_scr[...]
        for h in range(H_SB):
            o_ref[0, :, h * HEAD_DIM:(h + 1) * HEAD_DIM] = acc[h * nq:(h + 1) * nq,
                                                               h * HEAD_DIM:(h + 1) * HEAD_DIM]


def sb_sample(cache_sb, page_table, q_sb, kv_new):
    n_pool, page = cache_sb.shape[:2]
    bsz, n_pages = page_table.shape
    nq = q_sb.shape[1]
    pages = min(PAGES_PER_STEP_SB, n_pages)
    nsteps = n_pages // pages
    prow = page * 2 * H_SB
    cache = cache_sb.reshape(n_pool, prow, HEAD_DIM)
    qh = q_sb.reshape(bsz, nq, H_SB, HEAD_DIM).transpose(0, 2, 1, 3)
    qt = (qh[:, :, :, None, :] * jnp.eye(H_SB, dtype=BF16)[None, :, None, :, None])
    qt = qt.reshape(bsz, H_SB * nq, SB_W)
    new_page = jnp.pad(kv_new, ((0, 0), (0, page - nq), (0, 0)))

    def page_map(i):
        return lambda b, jb, pt: (pt[b, n_pages - (jb + 1) * pages + i], 0, 0)

    grid_spec = pltpu.PrefetchScalarGridSpec(
        num_scalar_prefetch=1,
        grid=(bsz, nsteps),
        in_specs=[pl.BlockSpec((1, prow, HEAD_DIM), page_map(i)) for i in range(pages)]
        + [pl.BlockSpec((1, H_SB * nq, SB_W), lambda b, jb, pt: (b, 0, 0)),
           pl.BlockSpec((1, page, 2 * SB_W), lambda b, jb, pt: (b, 0, 0))],
        out_specs=pl.BlockSpec((1, nq, SB_W), lambda b, jb, pt: (b, 0, 0)),
        scratch_shapes=[pltpu.VMEM((pages * page, SB_W), BF16), pltpu.VMEM((pages * page, SB_W), BF16),
                        pltpu.VMEM((H_SB * nq, SB_W), F32), pltpu.VMEM((H_SB * nq, 1), F32)],
    )
    return pl.pallas_call(
        functools.partial(_sb_sample_kernel, pages=pages, nq=nq),
        out_shape=jax.ShapeDtypeStruct((bsz, nq, SB_W), F32),
        grid_spec=grid_spec,
        compiler_params=_cparams(("arbitrary", "arbitrary")),
        name="sb_sample",
    )(page_table, *([cache] * pages), qt, new_page)


def _nsa_sample_sel_kernel(q_ref, kc_ref, vc_ref, st_ref, new_ref, ocmp_ref, owin_ref, selx_ref,
                           *, nq, nbp, past, topk):
    rows = HPG * nq
    slots = 2 * G_NSA
    wb = st_ref.shape[1] // slots
    t_in = lax.broadcasted_iota(jnp.int32, (nq, 1), 0)
    qpos4 = jnp.concatenate([past + t_in] * HPG, axis=0)
    kb = lax.broadcasted_iota(jnp.int32, (nbp, nbp * CMP_BLK), 0)
    kl = lax.broadcasted_iota(jnp.int32, (nbp, nbp * CMP_BLK), 1)
    expand = jnp.where(kl // CMP_BLK == kb, 1.0, 0.0).astype(BF16)
    lane_w = lax.broadcasted_iota(jnp.int32, (rows, wb), 1)
    lane_n = lax.broadcasted_iota(jnp.int32, (rows, new_ref.shape[1]), 1)
    for g in range(G_NSA):
        q4 = jnp.concatenate([q_ref[0, :, (g * HPG + n) * HEAD_DIM:(g * HPG + n + 1) * HEAD_DIM]
                              for n in range(HPG)], axis=0)
        o_cmp, rank = _cmp_branch(q4, qpos4, kc_ref[0, 0, g], vc_ref[0, 0, g], nq, nbp)
        sel = jnp.where(rank < topk, 1.0, 0.0).astype(BF16)
        selx_ref[0, g] = _dot(sel, expand)
        kcol, vcol = g * HEAD_DIM, (G_NSA + g) * HEAD_DIM
        state = _flash_init(rows)
        d = qpos4 - (past - wb + lane_w)
        state = _flash_tile(q4, st_ref[0, pl.ds(g, wb, stride=slots), :].astype(BF16),
                            st_ref[0, pl.ds(G_NSA + g, wb, stride=slots), :].astype(BF16),
                            (d >= 0) & (d < WINDOW), state)
        d = qpos4 - (past + lane_n)
        state = _flash_tile(q4, new_ref[0, :, kcol:kcol + HEAD_DIM].astype(BF16),
                            new_ref[0, :, vcol:vcol + HEAD_DIM].astype(BF16),
                            (d >= 0) & (d < WINDOW) & (lane_n < nq), state)
        o_win = _flash_out(state)
        for n in range(HPG):
            h = g * HPG + n
            ocmp_ref[0, :, h * HEAD_DIM:(h + 1) * HEAD_DIM] = o_cmp[n * nq:(n + 1) * nq]
            owin_ref[0, :, h * HEAD_DIM:(h + 1) * HEAD_DIM] = o_win[n * nq:(n + 1) * nq]


def nsa_sample_select(q_nsa, kvc, state_win, win_new, past):
    bsz, nq = q_nsa.shape[:2]
    nbp = kvc.shape[3]
    nb = -(-(past + nq) // CMP_BLK)
    wrows = state_win.shape[1]
    return pl.pallas_call(
        functools.partial(_nsa_sample_sel_kernel, nq=nq, nbp=nbp, past=past, topk=min(TOP_BLOCKS, nb)),
        out_shape=[jax.ShapeDtypeStruct((bsz, nq, NSA_W), F32),
                   jax.ShapeDtypeStruct((bsz, nq, NSA_W), F32),
                   jax.ShapeDtypeStruct((bsz, G_NSA, nq, nbp * CMP_BLK), F32)],
        grid=(bsz,),
        in_specs=[pl.BlockSpec((1, nq, NSA_W), lambda b: (b, 0, 0)),
                  pl.BlockSpec((1, 1, G_NSA, nbp, HEAD_DIM), lambda b: (b, 0, 0, 0, 0)),
                  pl.BlockSpec((1, 1, G_NSA, nbp, HEAD_DIM), lambda b: (b, 1, 0, 0, 0)),
                  pl.BlockSpec((1, wrows, HEAD_DIM), lambda b: (b, 0, 0)),
                  pl.BlockSpec((1,) + win_new.shape[1:], lambda b: (b, 0, 0))],
        out_specs=[pl.BlockSpec((1, nq, NSA_W), lambda b: (b, 0, 0)),
                   pl.BlockSpec((1, nq, NSA_W), lambda b: (b, 0, 0)),
                   pl.BlockSpec((1, G_NSA, nq, nbp * CMP_BLK), lambda b: (b, 0, 0, 0))],
        compiler_params=_cparams(("arbitrary",)),
        name="nsa_sample_select",
    )(q_nsa, kvc, kvc, state_win, win_new)


def _nsa_sample_slc_kernel(pt_ref, *refs, pages, nq, past):
    page_refs = refs[:pages]
    (q_ref, new_ref, selx_ref, selx_new_ref, ocmp_ref, owin_ref, gate_ref, o_ref,
     kbuf, vbuf, m_scr, l_scr, acc_scr) = refs[pages:]
    jb = pl.program_id(1)
    rows = HPG * nq
    slots = 2 * G_NSA
    page = page_refs[0].shape[1] // slots
    t_in = lax.broadcasted_iota(jnp.int32, (nq, 1), 0)
    qpos4 = jnp.concatenate([past + t_in] * HPG, axis=0)
    q4s = [jnp.concatenate([q_ref[0, :, (g * HPG + n) * HEAD_DIM:(g * HPG + n + 1) * HEAD_DIM]
                            for n in range(HPG)], axis=0) for g in range(G_NSA)]

    @pl.when(jb == 0)
    def _():
        m_scr[...] = jnp.full(m_scr.shape, NEG_INF, F32)
        l_scr[...] = jnp.zeros_like(l_scr)
        acc_scr[...] = jnp.zeros_like(acc_scr)

    def sweep(g, k, v, sx, causal):
        mask = jnp.concatenate([sx] * HPG, axis=0) > 0.5
        if causal is not None:
            mask = mask & causal
        state = (m_scr[g], l_scr[g], acc_scr[g])
        m_scr[g], l_scr[g], acc_scr[g] = _flash_tile(q4s[g], k, v, mask, state)

    for g in range(G_NSA):
        for i, pr in enumerate(page_refs):
            dst = slice(i * page, (i + 1) * page)
            kbuf[g, dst, :] = pr[0, pl.ds(g, page, stride=slots), :].astype(BF16)
            vbuf[g, dst, :] = pr[0, pl.ds(G_NSA + g, page, stride=slots), :].astype(BF16)
        sweep(g, kbuf[g], vbuf[g], selx_ref[0, g], None)

    @pl.when(jb == pl.num_programs(1) - 1)
    def _():
        nrow = new_ref.shape[1]
        lane_n = lax.broadcasted_iota(jnp.int32, (rows, nrow), 1)
        for g in range(G_NSA):
            kcol, vcol = g * HEAD_DIM, (G_NSA + g) * HEAD_DIM
            sweep(g, new_ref[0, :, kcol:kcol + HEAD_DIM].astype(BF16),
                  new_ref[0, :, vcol:vcol + HEAD_DIM].astype(BF16),
                  selx_new_ref[0, g, :, 0:nrow], ((past + lane_n) <= qpos4) & (lane_n < nq))
        gates = gate_ref[0]
        for g in range(G_NSA):
            o_slc = _flash_out((m_scr[g], l_scr[g], acc_scr[g]))
            for n in range(HPG):
                h = g * HPG + n
                cols = slice(h * HEAD_DIM, (h + 1) * HEAD_DIM)
                o_ref[0, :, cols] = (gates[:, 3 * h:3 * h + 1] * ocmp_ref[0, :, cols]
                                     + gates[:, 3 * h + 1:3 * h + 2] * o_slc[n * nq:(n + 1) * nq]
                                     + gates[:, 3 * h + 2:3 * h + 3] * owin_ref[0, :, cols])


def nsa_sample_slc(cache_slc, page_table, q_nsa, slc_new, selx, o_cmp, o_win, gates):
    n_pool, page = cache_slc.shape[:2]
    bsz, n_pages = page_table.shape
    nq = q_nsa.shape[1]
    past = n_pages * page
    pages = min(PAGES_PER_STEP_NSA, n_pages)
    prow = page * 2 * G_NSA
    cache = cache_slc.reshape(n_pool, prow, HEAD_DIM)
    rows = HPG * nq

    def page_map(i):
        return lambda b, jb, pt: (pt[b, jb * pages + i], 0, 0)

    per_b = lambda b, jb, pt: (b, 0, 0)
    grid_spec = pltpu.PrefetchScalarGridSpec(
        num_scalar_prefetch=1,
        grid=(bsz, n_pages // pages),
        in_specs=[pl.BlockSpec((1, prow, HEAD_DIM), page_map(i)) for i in range(pages)]
        + [pl.BlockSpec((1, nq, NSA_W), per_b),
           pl.BlockSpec((1,) + slc_new.shape[1:], per_b),
           pl.BlockSpec((1, G_NSA, nq, pages * page), lambda b, jb, pt: (b, 0, 0, jb)),
           pl.BlockSpec((1, G_NSA, nq, LANES), lambda b, jb, pt: (b, 0, 0, past // LANES)),
           pl.BlockSpec((1, nq, NSA_W), per_b),
           pl.BlockSpec((1, nq, NSA_W), per_b),
           pl.BlockSpec((1, nq, GATE_W), per_b)],
        out_specs=pl.BlockSpec((1, nq, NSA_W), per_b),
        scratch_shapes=[pltpu.VMEM((G_NSA, pages * page, HEAD_DIM), BF16),
                        pltpu.VMEM((G_NSA, pages * page, HEAD_DIM), BF16),
                        pltpu.VMEM((G_NSA, rows, 1), F32), pltpu.VMEM((G_NSA, rows, 1), F32),
                        pltpu.VMEM((G_NSA, rows, HEAD_DIM), F32)],
    )
    return pl.pallas_call(
        functools.partial(_nsa_sample_slc_kernel, pages=pages, nq=nq, past=past),
        out_shape=jax.ShapeDtypeStruct((bsz, nq, NSA_W), F32),
        grid_spec=grid_spec,
        compiler_params=_cparams(("arbitrary", "arbitrary")),
        name="nsa_sample_slc",
    )(page_table, *([cache] * pages), q_nsa, slc_new, selx, selx, o_cmp, o_win, gates)


def _outproj_kernel(osb_ref, onsa_ref, gsb_ref, gnsa_ref, wa_ref, wb_ref, x_ref, gate_ref,
                    g2_ref, shift_ref, scale_ref, x1_ref, h2_ref):
    a = _rms(osb_ref[...], gsb_ref[...]).astype(BF16)
    b = _rms(onsa_ref[...], gnsa_ref[...]).astype(BF16)
    mixed = _dot(a, wa_ref[...]) + _dot(b, wb_ref[...])
    x1 = x_ref[...] + gate_ref[0] * mixed
    x1_ref[...] = x1
    h2_ref[...] = (_rms(x1, g2_ref[...]) * (1.0 + scale_ref[0]) + shift_ref[0]).astype(BF16)


def out_projection(o_sb, o_nsa, g_sb, g_nsa, w_out_bf, x2d, gate, g2, shift, scale, tm, mod_map):
    r, d = x2d.shape
    mod_block = (1,) + gate.shape[1:]
    mod_spec = pl.BlockSpec(mod_block, lambda i: (mod_map(i), 0, 0))
    row = lambda w: pl.BlockSpec((tm, w), lambda i: (i, 0))
    const = lambda shp: pl.BlockSpec(shp, lambda i: (0,) * len(shp))
    return pl.pallas_call(
        _outproj_kernel,
        out_shape=[jax.ShapeDtypeStruct((r, d), F32), jax.ShapeDtypeStruct((r, d), BF16)],
        grid=(r // tm,),
        in_specs=[row(SB_W), row(NSA_W), const((1, SB_W)), const((1, NSA_W)),
                  pl.BlockSpec((SB_W, d), lambda i: (0, 0)),
                  pl.BlockSpec((NSA_W, d), lambda i: (1, 0)),
                  row(d), mod_spec, const((1, d)), mod_spec, mod_spec],
        out_specs=[row(d), row(d)],
        compiler_params=_cparams(("arbitrary",)),
        name="out_projection",
    )(o_sb, o_nsa, g_sb, g_nsa, w_out_bf, w_out_bf, x2d, gate, g2, shift, scale)


def _peer_score_kernel(h_ref, wq_ref, k1_ref, k2_ref, s_ref):
    q = _dot(h_ref[...], wq_ref[...]).astype(BF16)
    half = D_QUERY // 2
    nchunk = h_ref.shape[0] // LANES
    for h in range(PEER_HEADS):
        for c, k_ref in enumerate((k1_ref, k2_ref)):
            qh = q[:, h * D_QUERY + c * half:h * D_QUERY + (c + 1) * half]
            st = _dot_nt(k_ref[h], qh)
            for ch in range(nchunk):
                s_ref[c, h, ch] = st[:, ch * LANES:(ch + 1) * LANES]


def peer_scores(h2, wq_bf, k1_bf, k2_bf):
    r, d = h2.shape
    tm = PEER_TN
    nchunk = tm // LANES
    return pl.pallas_call(
        _peer_score_kernel,
        out_shape=jax.ShapeDtypeStruct((2, PEER_HEADS, r // LANES, N_KEYS, LANES), F32),
        grid=(r // tm,),
        in_specs=[pl.BlockSpec((tm, d), lambda i: (i, 0)),
                  pl.BlockSpec(wq_bf.shape, lambda i: (0, 0)),
                  pl.BlockSpec(k1_bf.shape, lambda i: (0, 0, 0)),
                  pl.BlockSpec(k2_bf.shape, lambda i: (0, 0, 0))],
        out_specs=pl.BlockSpec((2, PEER_HEADS, nchunk, N_KEYS, LANES), lambda i: (0, 0, i, 0, 0)),
        compiler_params=_cparams(("arbitrary",)),
        name="peer_scores",
    )(h2, wq_bf, k1_bf, k2_bf)


def _top_values(x, n, scr):
    rank = jnp.full(x.shape, float(n), F32)
    for it in range(n):
        m = jnp.max(x, axis=0, keepdims=True)
        scr[it:it + 1, :] = m
        hit = x == m
        rank = jnp.where(hit, float(it), rank)
        x = jnp.where(hit, NEG_INF, x)
    return scr[0:n, :], rank


def _pack_bf16(x):
    return pltpu.bitcast(x.astype(BF16), jnp.int32)


def _peer_select_kernel(s_ref, r2_ref, e2_ref, cnt_ref, e1_ref, v1_scr, v2_scr):
    k = PEER_TOPK
    nchunk = s_ref.shape[2]

    def body(idx, _):
        h, ch = idx // nchunk, idx % nchunk
        s1 = s_ref[0, h, ch]
        s2 = s_ref[1, h, ch]
        v1, r1 = _top_values(s1, k, v1_scr)
        v2, r2 = _top_values(s2, k, v2_scr)
        x = jnp.concatenate([v1[0:1] + v2] + [v1[i:i + 1] + v2[0:k // 2] for i in range(1, k // 2)]
                            + [v1[k // 2:k] + v2[0:1]], axis=0)
        xs = x
        for _ in range(k - 1):
            xs = jnp.where(xs == jnp.max(xs, axis=0, keepdims=True), NEG_INF, xs)
        tau = jnp.max(xs, axis=0, keepdims=True)
        vmax = v1[0:1] + v2[0:1]
        chosen = x >= tau
        z = jnp.sum(jnp.where(chosen, jnp.exp(x - vmax), 0.0), axis=0, keepdims=True)
        c = jnp.where(chosen, 1.0, 0.0)
        cnt_rows = ([jnp.sum(c[0:k], axis=0, keepdims=True)]
                    + [jnp.sum(c[k + (k // 2) * (i - 1):k + (k // 2) * i], axis=0, keepdims=True)
                       for i in range(1, k // 2)])
        cnt_tab = jnp.concatenate(cnt_rows + [c[k + (k // 2) * (k // 2 - 1):]], axis=0)
        cnt = jnp.zeros_like(s1)
        for i in range(k):
            cnt = jnp.where(r1 == float(i), cnt_tab[i:i + 1], cnt)
        r2_ref[h, ch] = _pack_bf16(r2)
        e2_ref[h, ch] = _pack_bf16(jnp.exp(s2 - v2[0:1]))
        cnt_ref[h, ch] = cnt
        e1_ref[h, ch] = jnp.exp(s1 - v1[0:1]) / z
        return 0

    lax.fori_loop(0, PEER_HEADS * nchunk, body, 0)


def peer_select(s):
    _, heads, nch_all, keys, lanes = s.shape
    nchunk = PEER_TN // LANES
    blk = (heads, nchunk, keys, lanes)
    pblk = (heads, nchunk, keys // 2, lanes)
    full = lambda shp: (shp[0], nch_all) + shp[2:]
    return pl.pallas_call(
        _peer_select_kernel,
        out_shape=[jax.ShapeDtypeStruct(full(pblk), jnp.int32),
                   jax.ShapeDtypeStruct(full(pblk), jnp.int32),
                   jax.ShapeDtypeStruct(full(blk), F32),
                   jax.ShapeDtypeStruct(full(blk), F32)],
        grid=(nch_all // nchunk,),
        in_specs=[pl.BlockSpec((2,) + blk, lambda i: (0, 0, i, 0, 0))],
        out_specs=[pl.BlockSpec(pblk, lambda i: (0, i, 0, 0)),
                   pl.BlockSpec(pblk, lambda i: (0, i, 0, 0)),
                   pl.BlockSpec(blk, lambda i: (0, i, 0, 0)),
                   pl.BlockSpec(blk, lambda i: (0, i, 0, 0))],
        scratch_shapes=[pltpu.VMEM((PEER_TOPK, LANES), F32), pltpu.VMEM((PEER_TOPK, LANES), F32)],
        compiler_params=_cparams(("arbitrary",)),
        name="peer_select",
    )(s)


def _row_bf16(row):
    packed_rows = 2 * SUBLANES
    return pltpu.repeat(jnp.broadcast_to(row, (packed_rows, LANES)).astype(BF16), N_KEYS // packed_rows, axis=0)


def _peer_expert_kernel(h_ref, u_ref, vt_ref, r2_ref, e2_ref, cnt_ref, e1_ref, o_ref, ga0, ga1, g0, g1):
    e = pl.program_id(1)
    ne = pl.num_programs(1) - 2
    nchunk = h_ref.shape[0] // LANES

    @pl.when(e == 0)
    def _():
        o_ref[...] = jnp.zeros_like(o_ref)
        ga0[...] = jnp.zeros_like(ga0)
        g1[...] = jnp.zeros_like(g1)

    def step(g_cur, g_prev, ga_prev, ga_prev2):
        et = jnp.minimum(e, ne - 1)
        zero = jnp.zeros((N_KEYS, LANES), BF16)
        for ch in range(nchunk):
            for al in range(PEER_AT):
                a = et * PEER_AT + al
                gsum = zero
                for h in range(PEER_HEADS):
                    hit = pltpu.bitcast(r2_ref[h, ch], BF16) < _row_bf16(cnt_ref[h, ch, pl.ds(a, 1), :])
                    e2 = pltpu.bitcast(e2_ref[h, ch], BF16)
                    gsum = gsum + jnp.where(hit, e2, zero) * _row_bf16(e1_ref[h, ch, pl.ds(a, 1), :])
                g_cur[al * N_KEYS:(al + 1) * N_KEYS, ch * LANES:(ch + 1) * LANES] = gsum
        o_ref[...] += _dot(vt_ref[...], ga_prev2[...])
        act = _gelu(_dot_nt(u_ref[...], h_ref[...]))
        ga_prev[...] = act.astype(BF16) * g_prev[...]

    @pl.when(e % 2 == 0)
    def _():
        step(g0, g1, ga1, ga0)

    @pl.when(e % 2 == 1)
    def _():
        step(g1, g0, ga0, ga1)


def peer_experts(h2, u_bf, vt_bf, r2p, e2p, cnt, e1):
    r, d = h2.shape
    n_exp = u_bf.shape[0]
    tn = PEER_TN
    nchunk = tn // LANES
    et = PEER_AT * N_KEYS
    sblk = (PEER_HEADS, nchunk, N_KEYS, LANES)
    pblk = (PEER_HEADS, nchunk, N_KEYS // 2, LANES)
    ne = n_exp // et
    return pl.pallas_call(
        _peer_expert_kernel,
        out_shape=jax.ShapeDtypeStruct((d, r), F32),
        grid=(r // tn, ne + 2),
        in_specs=[pl.BlockSpec((tn, d), lambda t, e: (t, 0)),
                  pl.BlockSpec((et, d), lambda t, e: (jnp.clip(e - 1, 0, ne - 1), 0)),
                  pl.BlockSpec((d, et), lambda t, e: (0, jnp.clip(e - 2, 0, ne - 1))),
                  pl.BlockSpec(pblk, lambda t, e: (0, t, 0, 0)),
                  pl.BlockSpec(pblk, lambda t, e: (0, t, 0, 0)),
                  pl.BlockSpec(sblk, lambda t, e: (0, t, 0, 0)),
                  pl.BlockSpec(sblk, lambda t, e: (0, t, 0, 0))],
        out_specs=pl.BlockSpec((d, tn), lambda t, e: (0, t)),
        scratch_shapes=[pltpu.VMEM((et, tn), BF16), pltpu.VMEM((et, tn), BF16),
                        pltpu.VMEM((et, tn), BF16), pltpu.VMEM((et, tn), BF16)],
        compiler_params=_cparams(("arbitrary", "arbitrary")),
        name="peer_experts",
    )(h2, u_bf, vt_bf, r2p, e2p, cnt, e1)


def _final_kernel(x1_ref, pt_ref, gate_ref, g_ref, y_ref):
    x2 = x1_ref[...] + gate_ref[0] * pt_ref[...].T
    y_ref[...] = _rms(x2, g_ref[...])


def final_norm(x1, peer_t, col0, gate, g, tm, mod_map):
    r, d = x1.shape
    mod_block = (1,) + gate.shape[1:]
    return pl.pallas_call(
        _final_kernel,
        out_shape=jax.ShapeDtypeStruct((r, d), F32),
        grid=(r // tm,),
        in_specs=[pl.BlockSpec((tm, d), lambda i: (i, 0)),
                  pl.BlockSpec((d, tm), lambda i: (0, col0 // tm + i)),
                  pl.BlockSpec(mod_block, lambda i: (mod_map(i), 0, 0)),
                  pl.BlockSpec((1, d), lambda i: (0, 0))],
        out_specs=pl.BlockSpec((tm, d), lambda i: (i, 0)),
        compiler_params=_cparams(("arbitrary",)),
        name="final_norm",
    )(x1, peer_t, gate, g)


def _layer(x_prompt, x_sample, cache_sb, cache_cmp, cache_slc, state_win, page_table, c_prompt,
           c_sample, w_ada, b_ada, norm1_g, w_in, cmp_pe_k, cmp_w1_k, cmp_w2_k, cmp_pe_v, cmp_w1_v,
           cmp_w2_v, out_g_sb, out_g_nsa, w_out, norm2_g, peer_wq, peer_k1, peer_k2, peer_u, peer_v):
    b, t, d = x_prompt.shape
    bs, nq, _ = x_sample.shape
    n_pages, page = page_table.shape[1], cache_sb.shape[1]
    past = n_pages * page
    rp, rs = b * t, bs * nq

    w_in_bf = jnp.pad(w_in, ((0, 0), (0, IN_NT * IN_TN - D_IN))).astype(BF16)
    w_out_bf = w_out.astype(BF16)
    wq_bf = peer_wq.astype(BF16)
    k1_bf, k2_bf = peer_k1.astype(BF16), peer_k2.astype(BF16)
    u_bf = peer_u.astype(BF16)
    vt_bf = peer_v.astype(BF16).T
    pe = jnp.stack([cmp_pe_k, cmp_pe_v])
    w1 = jnp.stack([cmp_w1_k, cmp_w1_v]).astype(BF16).reshape(2, CMP_BLK * HEAD_DIM, HEAD_DIM)
    w2 = jnp.stack([cmp_w2_k, cmp_w2_v]).astype(BF16)
    row = lambda v: v.reshape(1, -1)

    n_c = b + bs
    c_all = jnp.pad(jnp.concatenate([c_prompt, c_sample]), ((0, -n_c % 16), (0, 0))).astype(BF16)
    mod = adaln_mod(c_all, w_ada, row(b_ada))
    mod_p = [mod[:b, k * d:(k + 1) * d].reshape(b, 1, d) for k in range(6)]
    mod_s = [jnp.repeat(mod[b:n_c, k * d:(k + 1) * d], nq, axis=0).reshape(1, rs, d) for k in range(6)]

    tm_p = 512 if t % 512 == 0 else t
    xp = x_prompt.reshape(rp, d)
    tabs_p = _rope_tables(jnp.arange(t))
    qsb_p, kvsb_p, qnsa_p, kvcmp_p, kvslc_p, kvwin_p, gates_p = in_projection(
        xp, row(norm1_g), mod_p[0], mod_p[1], w_in_bf, tabs_p, tm_p,
        lambda i: i // (t // tm_p), lambda i: i % (t // tm_p))
    osb_p = sb_prompt(qsb_p, kvsb_p, b, t)
    kvc_p = compress_prompt(kvcmp_p, pe, w1, w2, b, t)
    onsa_p = nsa_prompt(qnsa_p, kvc_p, kvslc_p, kvwin_p, gates_p, b, t)
    tm_o = 256
    x1_p, h2_p = out_projection(osb_p, onsa_p, row(out_g_sb), row(out_g_nsa), w_out_bf, xp,
                                mod_p[2], row(norm2_g), mod_p[3], mod_p[4], tm_o,
                                lambda i: i // (t // tm_o))

    xs = x_sample.reshape(rs, d)
    tabs_s = _rope_tables(past + jnp.tile(jnp.arange(nq), bs))
    qsb_s, kvsb_s, qnsa_s, kvcmp_s, kvslc_s, kvwin_s, gates_s = in_projection(
        xs, row(norm1_g), mod_s[0], mod_s[1], w_in_bf, tabs_s, rs, lambda i: 0, lambda i: 0)
    osb_s = sb_sample(cache_sb, page_table, qsb_s.reshape(bs, nq, SB_W), kvsb_s.reshape(bs, nq, 2 * SB_W))
    kvcmp_s3 = kvcmp_s.reshape(bs, nq, KV_W)
    kvc_s = compress_sample(cache_cmp, page_table, kvcmp_s3, pe, w1, w2)
    pad8 = lambda a: jnp.pad(a.reshape(bs, nq, KV_W), ((0, 0), (0, -nq % SUBLANES), (0, 0)))
    qnsa_s3 = qnsa_s.reshape(bs, nq, NSA_W)
    st_win = state_win.reshape(bs, state_win.shape[1] * 2 * G_NSA, HEAD_DIM)
    ocmp_s, owin_s, selx = nsa_sample_select(qnsa_s3, kvc_s, st_win, pad8(kvwin_s), past)
    onsa_s = nsa_sample_slc(cache_slc, page_table, qnsa_s3, pad8(kvslc_s), selx, ocmp_s, owin_s,
                            gates_s.reshape(bs, nq, GATE_W))
    x1_s, h2_s = out_projection(osb_s.reshape(rs, SB_W), onsa_s.reshape(rs, NSA_W), row(out_g_sb),
                                row(out_g_nsa), w_out_bf, xs, mod_s[2], row(norm2_g), mod_s[3],
                                mod_s[4], rs, lambda i: 0)

    r = rp + rs
    r_pad = -(-r // PEER_TN) * PEER_TN
    h2 = jnp.pad(jnp.concatenate([h2_p, h2_s]), ((0, r_pad - r), (0, 0)))
    s = peer_scores(h2, wq_bf, k1_bf, k2_bf)
    r2p, e2p, cnt, e1 = peer_select(s)
    peer_t = peer_experts(h2, u_bf, vt_bf, r2p, e2p, cnt, e1)

    states_p = (kvsb_p.reshape(b, t, 2, H_SB, HEAD_DIM), kvcmp_p.reshape(b, t, 2, G_NSA, HEAD_DIM),
                kvslc_p.reshape(b, t, 2, G_NSA, HEAD_DIM),
                kvwin_p.reshape(b, t, 2, G_NSA, HEAD_DIM)[:, t - min(WINDOW, t):])
    kvwin_s5 = kvwin_s.reshape(bs, nq, 2, G_NSA, HEAD_DIM)
    states_s = (kvsb_s.reshape(bs, nq, 2, H_SB, HEAD_DIM), kvcmp_s.reshape(bs, nq, 2, G_NSA, HEAD_DIM),
                kvslc_s.reshape(bs, nq, 2, G_NSA, HEAD_DIM),
                jnp.concatenate([state_win, kvwin_s5], axis=1)[:, nq:])
    return (x1_p, mod_p[5], x1_s, mod_s[5], peer_t, rp), states_p, states_s


def kernel(x_prompt, x_sample, cache_sb, cache_cmp, cache_slc, state_win, page_table, c_prompt, c_sample, w_ada, b_ada, norm1_g, w_in, cmp_pe_k, cmp_w1_k, cmp_w2_k, cmp_pe_v, cmp_w1_v, cmp_w2_v, out_g_sb, out_g_nsa, w_out, norm2_g, peer_wq, peer_k1, peer_k2, peer_u, peer_v, final_g):
    depth = w_ada.shape[0]
    assert depth == 1, "single-layer trunk"
    b, t, d = x_prompt.shape
    bs, nq, _ = x_sample.shape
    (x1_p, gate_p, x1_s, gate_s, peer_t, rp), st_p, st_s = _layer(
        x_prompt, x_sample, cache_sb[0], cache_cmp[0], cache_slc[0], state_win[0], page_table,
        c_prompt, c_sample, w_ada[0], b_ada[0], norm1_g[0], w_in[0], cmp_pe_k[0], cmp_w1_k[0],
        cmp_w2_k[0], cmp_pe_v[0], cmp_w1_v[0], cmp_w2_v[0], out_g_sb[0], out_g_nsa[0], w_out[0],
        norm2_g[0], peer_wq[0], peer_k1[0], peer_k2[0], peer_u[0], peer_v[0])
    fg = final_g.reshape(1, d)
    tm_f = 256
    y_p = final_norm(x1_p, peer_t, 0, gate_p, fg, tm_f, lambda i: i // (t // tm_f)).reshape(b, t, d)
    rs = bs * nq
    y_s = final_norm(x1_s, peer_t, rp, gate_s, fg, rs, lambda i: 0).reshape(bs, nq, d)
    return (y_p, y_s, st_p[0][None], st_s[0][None], st_p[1][None], st_s[1][None],
            st_p[2][None], st_s[2][None], st_p[3][None], st_s[3][None])
```

```python
import functools

import jax
import jax.numpy as jnp
from jax import lax
from jax.experimental import pallas as pl
from jax.experimental.pallas import tpu as pltpu

F32 = jnp.float32
BF16 = jnp.bfloat16

HEAD_DIM = 128
H_SB = 8
H_NSA = 8
G_NSA = 2
HPG = H_NSA // G_NSA
SB_W = H_SB * HEAD_DIM
NSA_W = H_NSA * HEAD_DIM
KV_W = 2 * G_NSA * HEAD_DIM
D_IN = 3 * SB_W + NSA_W + 3 * KV_W + 3 * H_NSA
SCALE = HEAD_DIM ** -0.5
ROT_HALF = HEAD_DIM // 8
ROPE_THETA = 500000.0
CMP_BLK = 64
TOP_BLOCKS = 16
WINDOW = 512
FORCE_SCORE = 1e9
N_KEYS = 128
PEER_HEADS = 8
PEER_TOPK = 16
D_QUERY = 256
EPS = 1e-6
NEG_INF = -1e30

LANES = 128
SUBLANES = 8
VMEM_LIMIT = 56 * 1024 * 1024

MXU_N = 256
IN_TN = 2 * MXU_N
IN_NT = -(-D_IN // IN_TN)
GATE_W = LANES
PEER_TN = 3 * MXU_N
CMP_PITCH = CMP_BLK * 2 * G_NSA + SUBLANES
PEER_AT = 4
PAGES_PER_STEP_SB = 8
PAGES_PER_STEP_NSA = 16


def _cparams(sem):
    return pltpu.CompilerParams(dimension_semantics=sem, vmem_limit_bytes=VMEM_LIMIT)


def _dot(a, b):
    return jnp.dot(a, b, preferred_element_type=F32)


def _dot_nt(a, b):
    return lax.dot_general(a, b, (((1,), (1,)), ((), ())), preferred_element_type=F32)


def _dot_tn(a, b):
    return lax.dot_general(a, b, (((0,), (0,)), ((), ())), preferred_element_type=F32)


def _gelu(x):
    return 0.5 * x * (1.0 + jnp.tanh(0.7978845608028654 * (x + 0.044715 * (x * x * x))))


def _rms(x, g):
    return x * lax.rsqrt(jnp.mean(x * x, axis=-1, keepdims=True) + EPS) * g


def _softplus(z):
    return jnp.maximum(z, 0.0) + jnp.log(1.0 + jnp.exp(-jnp.abs(z)))


def _split_bf16(x):
    hi = x.astype(BF16)
    lo = (x - hi.astype(F32)).astype(BF16)
    return hi, lo


def _mod_kernel(c_ref, w_ref, b_ref, o_ref):
    o_ref[...] = _dot(c_ref[...], w_ref[...].astype(BF16)) + b_ref[...]


def adaln_mod(c_bf, w_ada, b_ada):
    m, d = c_bf.shape
    n = w_ada.shape[1]
    tn = 1024
    return pl.pallas_call(
        _mod_kernel,
        out_shape=jax.ShapeDtypeStruct((m, n), F32),
        grid=(n // tn,),
        in_specs=[pl.BlockSpec((m, d), lambda j: (0, 0)),
                  pl.BlockSpec((d, tn), lambda j: (0, j)),
                  pl.BlockSpec((1, tn), lambda j: (0, j))],
        out_specs=pl.BlockSpec((m, tn), lambda j: (0, j)),
        compiler_params=_cparams(("arbitrary",)),
        name="adaln_mod",
    )(c_bf, w_ada, b_ada)


_IN_SEGMENTS = (
    (0, 2, 0, "plain"),
    (2, 6, 1, "plain"),
    (6, 8, 2, "rope"),
    (8, 9, 3, "rope_k"),
    (9, 10, 4, "rope_k"),
    (10, 11, 5, "rope_k"),
    (11, 12, 6, "gate"),
)
_IN_OUT_RANGES = ((0, 2), (2, 6), (6, 8), (8, 9), (9, 10), (10, 11), (11, 12))


def _rope(p, rc, ra, rb):
    w = p.shape[1]
    return p * rc + pltpu.roll(p, ROT_HALF, 1) * ra + pltpu.roll(p, w - ROT_HALF, 1) * rb


def _inproj_kernel(x_ref, g_ref, shift_ref, scale_ref, w_ref, rc_ref, ra_ref, rb_ref,
                   *rest):
    outs, h_scr = rest[:7], rest[7]
    j = pl.program_id(1)

    @pl.when(j == 0)
    def _():
        h = _rms(x_ref[...], g_ref[...]) * (1.0 + scale_ref[0]) + shift_ref[0]
        h_scr[...] = h.astype(BF16)

    p = _dot(h_scr[...], w_ref[...])
    kw = G_NSA * HEAD_DIM
    for lo, hi, slot, kind in _IN_SEGMENTS:
        @pl.when((j >= lo) & (j < hi))
        def _(slot=slot, kind=kind):
            if kind == "rope":
                v = _rope(p, rc_ref[...], ra_ref[...], rb_ref[...])
            elif kind == "rope_k":
                kk = _rope(p[:, :kw], rc_ref[:, :kw], ra_ref[:, :kw], rb_ref[:, :kw])
                nslot = 2 * G_NSA
                for s in range(nslot):
                    src = kk if s < G_NSA else p
                    outs[slot][pl.ds(s, p.shape[0], stride=nslot), :] = src[:, s * HEAD_DIM:(s + 1) * HEAD_DIM]
                return
            elif kind == "gate":
                v = jax.nn.sigmoid(p[:, :GATE_W])
            else:
                v = p
            outs[slot][...] = v.astype(outs[slot].dtype)


def _rope_tables(pos):
    inv = jnp.power(ROPE_THETA, -jnp.arange(ROT_HALF, dtype=F32) / ROT_HALF)
    ang = pos.astype(F32)[:, None] * inv[None, :]
    cos, sin = jnp.cos(ang), jnp.sin(ang)
    n = pos.shape[0]
    z16 = jnp.zeros((n, ROT_HALF), F32)
    tail0 = jnp.zeros((n, HEAD_DIM - 2 * ROT_HALF), F32)
    rc = jnp.concatenate([cos, cos, tail0 + 1.0], axis=1)
    ra = jnp.concatenate([z16, sin, tail0], axis=1)
    rb = jnp.concatenate([-sin, z16, tail0], axis=1)
    heads = IN_TN // HEAD_DIM
    return tuple(jnp.tile(t, (1, heads)) for t in (rc, ra, rb))


def in_projection(x2d, g1, shift, scale, w_bf, tables, tm, mod_map, tab_map):
    r, d = x2d.shape
    nslot = 2 * G_NSA
    wide = lambda lo, hi: ((r, (hi - lo) * IN_TN), (tm, IN_TN))
    flat = ((r * nslot, HEAD_DIM), (tm * nslot, HEAD_DIM))
    shapes = ([wide(*_IN_OUT_RANGES[k]) for k in range(3)] + [flat] * 3 + [((r, GATE_W), (tm, GATE_W))])
    dtypes = [BF16, F32, BF16, F32, F32, F32, F32]
    out_shape = [jax.ShapeDtypeStruct(s, dt) for (s, _), dt in zip(shapes, dtypes)]

    def out_map(lo, hi):
        return lambda i, j: (i, jnp.clip(j - lo, 0, hi - lo - 1))

    mod_block = (1,) + shift.shape[1:]
    return pl.pallas_call(
        _inproj_kernel,
        out_shape=out_shape,
        grid=(r // tm, IN_NT),
        in_specs=[pl.BlockSpec((tm, d), lambda i, j: (i, 0)),
                  pl.BlockSpec((1, d), lambda i, j: (0, 0)),
                  pl.BlockSpec(mod_block, lambda i, j: (mod_map(i), 0, 0)),
                  pl.BlockSpec(mod_block, lambda i, j: (mod_map(i), 0, 0)),
                  pl.BlockSpec((d, IN_TN), lambda i, j: (0, j)),
                  pl.BlockSpec((tm, IN_TN), lambda i, j: (tab_map(i), 0)),
                  pl.BlockSpec((tm, IN_TN), lambda i, j: (tab_map(i), 0)),
                  pl.BlockSpec((tm, IN_TN), lambda i, j: (tab_map(i), 0))],
        out_specs=[pl.BlockSpec(blk, out_map(lo, hi))
                   for (_, blk), (lo, hi) in zip(shapes, _IN_OUT_RANGES)],
        scratch_shapes=[pltpu.VMEM((tm, d), BF16)],
        compiler_params=_cparams(("arbitrary", "arbitrary")),
        name="in_projection",
    )(x2d, g1, shift, scale, w_bf, *tables)


def _sb_weights(z, valid, later, tri):
    rows, keys = z.shape
    sub = tri.shape[0]
    nsub = keys // sub
    sp = _softplus(z)
    log1m = -sp if valid is None else jnp.where(valid, -sp, 0.0)
    stacked = jnp.concatenate([log1m[:, i * sub:(i + 1) * sub] for i in range(nsub)], axis=0)
    hi, lo = _split_bf16(stacked)
    within = _dot(hi, tri) + _dot(lo, tri)
    afters = [None] * nsub
    for i in reversed(range(nsub)):
        w = within[i * rows:(i + 1) * rows]
        afters[i] = w + later
        later = later + w[:, 0:1] + log1m[:, i * sub:i * sub + 1]
    a = jnp.exp(z - sp + jnp.concatenate(afters, axis=1))
    if valid is not None:
        a = jnp.where(valid, a, 0.0)
    return a, later


def _later_keys_tri(sub):
    return (lax.broadcasted_iota(jnp.int32, (sub, sub), 0)
            > lax.broadcasted_iota(jnp.int32, (sub, sub), 1)).astype(BF16)


def _sb_prompt_kernel(q_ref, k_ref, v_ref, o_ref, *, tq, tk, nh):
    qi = pl.program_id(2)
    qpos = qi * tq + lax.broadcasted_iota(jnp.int32, (tq, tk), 0)
    lane = lax.broadcasted_iota(jnp.int32, (tq, tk), 1)
    tri = _later_keys_tri(LANES)
    nk = ((qi + 1) * tq + tk - 1) // tk
    heads = [slice(hh * HEAD_DIM, (hh + 1) * HEAD_DIM) for hh in range(nh)]
    qs = [q_ref[:, cols] for cols in heads]

    def body(it, carry):
        start = pl.multiple_of((nk - 1 - it) * tk, tk)
        valid = (start + lane) < qpos
        out = []
        for (later, acc), q, cols in zip(carry, qs, heads):
            k = k_ref[pl.ds(start, tk), cols].astype(BF16)
            v = v_ref[pl.ds(start, tk), cols].astype(BF16)
            a, later = _sb_weights(_dot_nt(q, k) * SCALE, valid, later, tri)
            out.append((later, acc + _dot(a.astype(BF16), v)))
        return tuple(out)

    init = tuple((jnp.zeros((tq, 1), F32), jnp.zeros((tq, HEAD_DIM), F32)) for _ in heads)
    for (_, acc), cols in zip(lax.fori_loop(0, nk, body, init), heads):
        o_ref[:, cols] = acc


def sb_prompt(q_sb, kv_sb, b, t):
    tq = 256
    tk = 512 if t % 512 == 0 else tq
    nq = t // tq
    nh = 2
    hw = nh * HEAD_DIM
    return pl.pallas_call(
        functools.partial(_sb_prompt_kernel, tq=tq, tk=tk, nh=nh),
        out_shape=jax.ShapeDtypeStruct((b * t, SB_W), F32),
        grid=(b, H_SB // nh, nq),
        in_specs=[pl.BlockSpec((tq, hw), lambda bi, h, qi: (bi * nq + qi, h)),
                  pl.BlockSpec((t, hw), lambda bi, h, qi: (bi, h)),
                  pl.BlockSpec((t, hw), lambda bi, h, qi: (bi, H_SB // nh + h))],
        out_specs=pl.BlockSpec((tq, hw), lambda bi, h, qi: (bi * nq + qi, h)),
        compiler_params=_cparams(("arbitrary", "arbitrary", "arbitrary")),
        name="sb_prompt",
    )(q_sb, kv_sb, kv_sb)


def _compress_rows(read_rows, pe_ref, w1_ref, w2_ref, x_scr, c):
    for p in range(CMP_BLK):
        x = jnp.concatenate([read_rows(p, 0), read_rows(p, 1)], axis=0) + pe_ref[c, p:p + 1, :]
        x_scr[:, p * HEAD_DIM:(p + 1) * HEAD_DIM] = x.astype(BF16)
    hid = _gelu(_dot(x_scr[...], w1_ref[c]))
    return _dot(hid.astype(BF16), w2_ref[c])


def _cmp_prompt_kernel(x_ref, pe_ref, w1_ref, w2_ref, o_ref, x_scr, *, nb):
    stride = CMP_BLK * 2 * G_NSA
    for c in range(2):
        def read_rows(p, g, c=c):
            return x_ref[0, pl.ds(p * 2 * G_NSA + c * G_NSA + g, nb, stride=stride), :]
        out = _compress_rows(read_rows, pe_ref, w1_ref, w2_ref, x_scr, c)
        o_ref[0, c, 0] = out[:nb]
        o_ref[0, c, 1] = out[nb:]


def compress_prompt(kv_cmp, pe, w1, w2, b, t):
    nb = t // CMP_BLK
    x = kv_cmp.reshape(b, t * 2 * G_NSA, HEAD_DIM)
    return pl.pallas_call(
        functools.partial(_cmp_prompt_kernel, nb=nb),
        out_shape=jax.ShapeDtypeStruct((b, 2, G_NSA, nb, HEAD_DIM), F32),
        grid=(b,),
        in_specs=[pl.BlockSpec((1, t * 2 * G_NSA, HEAD_DIM), lambda i: (i, 0, 0)),
                  pl.BlockSpec(pe.shape, lambda i: (0, 0, 0)),
                  pl.BlockSpec(w1.shape, lambda i: (0, 0, 0)),
                  pl.BlockSpec(w2.shape, lambda i: (0, 0, 0))],
        out_specs=pl.BlockSpec((1, 2, G_NSA, nb, HEAD_DIM), lambda i: (i, 0, 0, 0, 0)),
        scratch_shapes=[pltpu.VMEM((G_NSA * nb, CMP_BLK * HEAD_DIM), BF16)],
        compiler_params=_cparams(("arbitrary",)),
        name="compress_prompt",
    )(x, pe, w1, w2)


def _cmp_sample_kernel(pt_ref, *refs, pages, nbp, past):
    page_refs = refs[:pages]
    new_ref, pe_ref, w1_ref, w2_ref, o_ref, buf, x_scr = refs[pages:]
    b, jb = pl.program_id(0), pl.program_id(1)
    blk_rows = CMP_BLK * 2 * G_NSA
    blks_per_page = page_refs[0].shape[1] // blk_rows
    past_blks = past // CMP_BLK

    @pl.when((b == 0) & (jb == 0))
    def _():
        tail = buf.shape[0] - past_blks * CMP_PITCH
        buf[pl.ds(past_blks * CMP_PITCH, tail), :] = jnp.zeros((tail, HEAD_DIM), F32)

    for i, pr in enumerate(page_refs):
        for hb in range(blks_per_page):
            blk = (jb * pages + i) * blks_per_page + hb
            start = pl.multiple_of(blk * CMP_PITCH, SUBLANES)
            buf[pl.ds(start, blk_rows), :] = pr[0, hb * blk_rows:(hb + 1) * blk_rows, :]

    @pl.when(jb == pl.num_programs(1) - 1)
    def _():
        buf[pl.ds(past_blks * CMP_PITCH, new_ref.shape[1]), :] = new_ref[0]
        for c in range(2):
            def read_rows(p, g, c=c):
                return buf[pl.ds(p * 2 * G_NSA + c * G_NSA + g, nbp, stride=CMP_PITCH), :]
            out = _compress_rows(read_rows, pe_ref, w1_ref, w2_ref, x_scr, c)
            o_ref[0, c, 0] = out[:nbp]
            o_ref[0, c, 1] = out[nbp:]


def compress_sample(cache_cmp, page_table, kv_new, pe, w1, w2):
    n_pool, page = cache_cmp.shape[:2]
    bsz, n_pages = page_table.shape
    nq = kv_new.shape[1]
    past = n_pages * page
    nb = -(-(past + nq) // CMP_BLK)
    nbp = -(-nb // SUBLANES) * SUBLANES
    pages = min(PAGES_PER_STEP_NSA, n_pages)
    rpp = page * 2 * G_NSA
    cache = cache_cmp.reshape(n_pool, rpp, HEAD_DIM)
    new = kv_new.reshape(bsz, nq * 2 * G_NSA, HEAD_DIM)

    def page_map(i):
        return lambda b, jb, pt: (pt[b, jb * pages + i], 0, 0)

    grid_spec = pltpu.PrefetchScalarGridSpec(
        num_scalar_prefetch=1,
        grid=(bsz, n_pages // pages),
        in_specs=[pl.BlockSpec((1, rpp, HEAD_DIM), page_map(i)) for i in range(pages)]
        + [pl.BlockSpec((1,) + new.shape[1:], lambda b, jb, pt: (b, 0, 0)),
           pl.BlockSpec(pe.shape, lambda b, jb, pt: (0, 0, 0)),
           pl.BlockSpec(w1.shape, lambda b, jb, pt: (0, 0, 0)),
           pl.BlockSpec(w2.shape, lambda b, jb, pt: (0, 0, 0))],
        out_specs=pl.BlockSpec((1, 2, G_NSA, nbp, HEAD_DIM), lambda b, jb, pt: (b, 0, 0, 0, 0)),
        scratch_shapes=[pltpu.VMEM((nbp * CMP_PITCH, HEAD_DIM), F32),
                        pltpu.VMEM((G_NSA * nbp, CMP_BLK * HEAD_DIM), BF16)],
    )
    return pl.pallas_call(
        functools.partial(_cmp_sample_kernel, pages=pages, nbp=nbp, past=past),
        out_shape=jax.ShapeDtypeStruct((bsz, 2, G_NSA, nbp, HEAD_DIM), F32),
        grid_spec=grid_spec,
        compiler_params=_cparams(("arbitrary", "arbitrary")),
        name="compress_sample",
    )(page_table, *([cache] * pages), new, pe, w1, w2)


def _cmp_branch(q4, qpos4, kc, vc, nq, nbp):
    blk = lax.broadcasted_iota(jnp.int32, (1, nbp), 1)
    s = _dot_nt(q4, kc.astype(BF16)) * SCALE
    complete = (blk + 1) * CMP_BLK <= qpos4 + 1
    m = jnp.max(jnp.where(complete, s, NEG_INF), axis=1, keepdims=True)
    e = jnp.where(complete, jnp.exp(s - m), 0.0)
    den = jnp.sum(e, axis=1, keepdims=True)
    p = e / jnp.where(den > 0.0, den, 1.0)
    o_cmp = _dot(p.astype(BF16), vc.astype(BF16))
    imp = p[0:nq]
    for n in range(1, HPG):
        imp = imp + p[n * nq:(n + 1) * nq]
    qpos = qpos4[0:nq]
    cur = qpos // CMP_BLK
    started = blk <= cur
    forced = started & ((blk == 0) | (blk >= cur - 1))
    score = jnp.where(forced, FORCE_SCORE, jnp.where(started, imp, -FORCE_SCORE))
    rank = jnp.zeros((nq, nbp), F32)
    for j in range(nbp):
        col = score[:, j:j + 1]
        rank = rank + jnp.where(col > score, 1.0, 0.0) + jnp.where((col == score) & (blk > j), 1.0, 0.0)
    return o_cmp, rank


def _cmp_branch_t(q4, qpos_row, kc, vc, nq, nb):
    blk = lax.broadcasted_iota(jnp.int32, (nb, 1), 0)
    s = _dot_nt(kc.astype(BF16), q4) * SCALE
    complete = (blk + 1) * CMP_BLK <= qpos_row + 1
    m = jnp.max(jnp.where(complete, s, NEG_INF), axis=0, keepdims=True)
    e = jnp.where(complete, jnp.exp(s - m), 0.0)
    den = jnp.sum(e, axis=0, keepdims=True)
    p = e / jnp.where(den > 0.0, den, 1.0)
    o_cmp = _dot_tn(p.astype(BF16), vc.astype(BF16))
    imp = p[:, 0:nq]
    for n in range(1, HPG):
        imp = imp + p[:, n * nq:(n + 1) * nq]
    cur = qpos_row[:, 0:nq] // CMP_BLK
    started = blk <= cur
    forced = started & ((blk == 0) | (blk >= cur - 1))
    score = jnp.where(forced, FORCE_SCORE, jnp.where(started, imp, -FORCE_SCORE))
    rank = jnp.zeros((nb, nq), F32)
    for j in range(nb):
        row = score[j:j + 1, :]
        rank = rank + jnp.where(row > score, 1.0, 0.0) + jnp.where((row == score) & (blk > j), 1.0, 0.0)
    return o_cmp, rank


def _flash_tile(q4, k, v, mask, state):
    m, l, acc = state
    s = _dot_nt(q4, k) * SCALE
    m_new = jnp.maximum(m, jnp.max(jnp.where(mask, s, NEG_INF), axis=1, keepdims=True))
    p = jnp.where(mask, jnp.exp(s - m_new), 0.0)
    alpha = jnp.exp(m - m_new)
    l = alpha * l + jnp.sum(p, axis=1, keepdims=True)
    acc = alpha * acc + _dot(p.astype(BF16), v)
    return m_new, l, acc


def _flash_init(rows):
    return (jnp.full((rows, 1), NEG_INF, F32), jnp.zeros((rows, 1), F32),
            jnp.zeros((rows, HEAD_DIM), F32))


def _flash_out(state):
    _, l, acc = state
    return acc / jnp.where(l > 0.0, l, 1.0)


def _stack_heads(q_ref):
    return jnp.concatenate([q_ref[:, n * HEAD_DIM:(n + 1) * HEAD_DIM] for n in range(HPG)], axis=0)


def _nsa_prompt_kernel(q_ref, kc_ref, vc_ref, slc_ref, win_ref, gate_ref, o_ref,
                       selx_scr, *, tq, tk, nb, topk):
    g, qi = pl.program_id(1), pl.program_id(2)
    rows = HPG * tq
    nslot = 2 * G_NSA
    t = slc_ref.shape[0] // nslot
    q4 = _stack_heads(q_ref)

    def keys(ref, start, n):
        return (ref[pl.ds(start * nslot + g, n, stride=nslot), :].astype(BF16),
                ref[pl.ds(start * nslot + G_NSA + g, n, stride=nslot), :].astype(BF16))
    t_in = lax.broadcasted_iota(jnp.int32, (tq, 1), 0)
    qpos4 = jnp.concatenate([qi * tq + t_in] * HPG, axis=0)
    qpos_row = qi * tq + lax.broadcasted_iota(jnp.int32, (1, rows), 1) % tq
    o_cmp, rank_t = _cmp_branch_t(q4, qpos_row, kc_ref[0, 0, 0], vc_ref[0, 0, 0], tq, nb)
    sel_t = jnp.where(rank_t < topk, 1.0, 0.0).astype(BF16)
    kb = lax.broadcasted_iota(jnp.int32, (nb, t), 0)
    kl = lax.broadcasted_iota(jnp.int32, (nb, t), 1)
    selx = _dot_tn(sel_t, jnp.where(kl // CMP_BLK == kb, 1.0, 0.0).astype(BF16))
    for j in range(t // tk):
        selx_scr[j] = selx[:, j * tk:(j + 1) * tk]
    lane = lax.broadcasted_iota(jnp.int32, (rows, tk), 1)

    def slc_body(j, state):
        start = pl.multiple_of(j * tk, tk)
        k, v = keys(slc_ref, start, tk)
        sx = selx_scr[j]
        mask = (jnp.concatenate([sx] * HPG, axis=0) > 0.5) & ((start + lane) <= qpos4)
        return _flash_tile(q4, k, v, mask, state)

    nk = ((qi + 1) * tq + tk - 1) // tk
    o_slc = _flash_out(lax.fori_loop(0, nk, slc_body, _flash_init(rows)))

    wlen = min(WINDOW + tq, t)
    start = pl.multiple_of(jnp.clip(qi * tq - WINDOW, 0, t - wlen), tq)
    d = qpos4 - (start + lax.broadcasted_iota(jnp.int32, (rows, wlen), 1))
    o_win = _flash_out(_flash_tile(q4, *keys(win_ref, start, wlen), (d >= 0) & (d < WINDOW),
                                   _flash_init(rows)))

    gates = gate_ref[...]

    def gate(n, branch):
        col = gates[:, n * 3 + branch:n * 3 + branch + 1]
        for gg in range(1, G_NSA):
            c = (gg * HPG + n) * 3 + branch
            col = jnp.where(g == gg, gates[:, c:c + 1], col)
        return col

    for n in range(HPG):
        r0 = n * tq
        o_ref[:, n * HEAD_DIM:(n + 1) * HEAD_DIM] = (
            gate(n, 0) * o_cmp[r0:r0 + tq] + gate(n, 1) * o_slc[r0:r0 + tq]
            + gate(n, 2) * o_win[r0:r0 + tq])


def nsa_prompt(q_nsa, kvc, kv_slc, kv_win, gates, b, t):
    tq = 128
    tk = 512 if t % 512 == 0 else tq
    nq = t // tq
    nb = t // CMP_BLK
    topk = min(TOP_BLOCKS, nb)
    gw = HPG * HEAD_DIM
    return pl.pallas_call(
        functools.partial(_nsa_prompt_kernel, tq=tq, tk=tk, nb=nb, topk=topk),
        out_shape=jax.ShapeDtypeStruct((b * t, NSA_W), F32),
        grid=(b, G_NSA, nq),
        in_specs=[pl.BlockSpec((tq, gw), lambda bi, g, qi: (bi * nq + qi, g)),
                  pl.BlockSpec((1, 1, 1, nb, HEAD_DIM), lambda bi, g, qi: (bi, 0, g, 0, 0)),
                  pl.BlockSpec((1, 1, 1, nb, HEAD_DIM), lambda bi, g, qi: (bi, 1, g, 0, 0)),
                  pl.BlockSpec((t * 2 * G_NSA, HEAD_DIM), lambda bi, g, qi: (bi, 0)),
                  pl.BlockSpec((t * 2 * G_NSA, HEAD_DIM), lambda bi, g, qi: (bi, 0)),
                  pl.BlockSpec((tq, GATE_W), lambda bi, g, qi: (bi * nq + qi, 0))],
        out_specs=pl.BlockSpec((tq, gw), lambda bi, g, qi: (bi * nq + qi, g)),
        scratch_shapes=[pltpu.VMEM((t // tk, tq, tk), F32)],
        compiler_params=_cparams(("arbitrary", "arbitrary", "arbitrary")),
        name="nsa_prompt",
    )(q_nsa, kvc, kvc, kv_slc, kv_win, gates)


def _sb_sample_kernel(pt_ref, *refs, pages, nq):
    k_refs, v_refs = refs[:pages], refs[pages:2 * pages]
    qt_ref, new_ref, o_ref, kbuf, vbuf, acc_scr, later_scr = refs[2 * pages:]
    jb = pl.program_id(1)
    page = new_ref.shape[1]
    rows = H_SB * nq
    tri = _later_keys_tri(LANES)
    qt = qt_ref[0]

    def sweep(k, v, valid):
        z = _dot_nt(qt, k) * SCALE
        a, later = _sb_weights(z, valid, later_scr[...], tri)
        acc_scr[...] += _dot(a.astype(BF16), v)
        later_scr[...] = later

    @pl.when(jb == 0)
    def _():
        acc_scr[...] = jnp.zeros_like(acc_scr)
        later_scr[...] = jnp.zeros_like(later_scr)
        key = lax.broadcasted_iota(jnp.int32, (rows, page), 1)
        qry = lax.broadcasted_iota(jnp.int32, (rows, page), 0) % nq
        sweep(new_ref[0, :, :SB_W].astype(BF16), new_ref[0, :, SB_W:].astype(BF16), key < qry)

    for src_refs, buf in ((k_refs, kbuf), (v_refs, vbuf)):
        for i, pr in enumerate(src_refs):
            by_head = jnp.swapaxes(pr[0, :, 0], 0, 1).astype(BF16)
            for h in range(H_SB):
                buf[i * page:(i + 1) * page, h * HEAD_DIM:(h + 1) * HEAD_DIM] = by_head[h]
    sweep(kbuf[...], vbuf[...], None)

    @pl.when(jb == pl.num_programs(1) - 1)
    def _():
        acc = acc_scr[...]
        for h in range(H_SB):
            o_ref[0, :, h * HEAD_DIM:(h + 1) * HEAD_DIM] = acc[h * nq:(h + 1) * nq,
                                                               h * HEAD_DIM:(h + 1) * HEAD_DIM]


def sb_sample(cache_sb, page_table, q_sb, kv_new):
    n_pool, page = cache_sb.shape[:2]
    bsz, n_pages = page_table.shape
    nq = q_sb.shape[1]
    pages = min(PAGES_PER_STEP_SB, n_pages)
    nsteps = n_pages // pages
    half = (1, page, 1, H_SB, HEAD_DIM)
    qh = q_sb.reshape(bsz, nq, H_SB, HEAD_DIM).transpose(0, 2, 1, 3)
    qt = (qh[:, :, :, None, :] * jnp.eye(H_SB, dtype=BF16)[None, :, None, :, None])
    qt = qt.reshape(bsz, H_SB * nq, SB_W)
    new_page = jnp.pad(kv_new, ((0, 0), (0, page - nq), (0, 0)))

    def page_map(i, c):
        return lambda b, jb, pt: (pt[b, n_pages - (jb + 1) * pages + i], 0, c, 0, 0)

    grid_spec = pltpu.PrefetchScalarGridSpec(
        num_scalar_prefetch=1,
        grid=(bsz, nsteps),
        in_specs=[pl.BlockSpec(half, page_map(i, c)) for c in range(2) for i in range(pages)]
        + [pl.BlockSpec((1, H_SB * nq, SB_W), lambda b, jb, pt: (b, 0, 0)),
           pl.BlockSpec((1, page, 2 * SB_W), lambda b, jb, pt: (b, 0, 0))],
        out_specs=pl.BlockSpec((1, nq, SB_W), lambda b, jb, pt: (b, 0, 0)),
        scratch_shapes=[pltpu.VMEM((pages * page, SB_W), BF16), pltpu.VMEM((pages * page, SB_W), BF16),
                        pltpu.VMEM((H_SB * nq, SB_W), F32), pltpu.VMEM((H_SB * nq, 1), F32)],
    )
    return pl.pallas_call(
        functools.partial(_sb_sample_kernel, pages=pages, nq=nq),
        out_shape=jax.ShapeDtypeStruct((bsz, nq, SB_W), F32),
        grid_spec=grid_spec,
        compiler_params=_cparams(("arbitrary", "arbitrary")),
        name="sb_sample",
    )(page_table, *([cache_sb] * (2 * pages)), qt, new_page)


def _nsa_sample_sel_kernel(q_ref, kc_ref, vc_ref, st_ref, new_ref, ocmp_ref, owin_ref, selx_ref,
                           *, nq, nbp, past, topk):
    rows = HPG * nq
    slots = 2 * G_NSA
    wb = st_ref.shape[1] // slots
    t_in = lax.broadcasted_iota(jnp.int32, (nq, 1), 0)
    qpos4 = jnp.concatenate([past + t_in] * HPG, axis=0)
    kb = lax.broadcasted_iota(jnp.int32, (nbp, nbp * CMP_BLK), 0)
    kl = lax.broadcasted_iota(jnp.int32, (nbp, nbp * CMP_BLK), 1)
    expand = jnp.where(kl // CMP_BLK == kb, 1.0, 0.0).astype(BF16)
    lane_w = lax.broadcasted_iota(jnp.int32, (rows, wb), 1)
    lane_n = lax.broadcasted_iota(jnp.int32, (rows, new_ref.shape[1]), 1)
    for g in range(G_NSA):
        q4 = jnp.concatenate([q_ref[0, :, (g * HPG + n) * HEAD_DIM:(g * HPG + n + 1) * HEAD_DIM]
                              for n in range(HPG)], axis=0)
        o_cmp, rank = _cmp_branch(q4, qpos4, kc_ref[0, 0, g], vc_ref[0, 0, g], nq, nbp)
        sel = jnp.where(rank < topk, 1.0, 0.0).astype(BF16)
        selx_ref[0, g] = _dot(sel, expand)
        kcol, vcol = g * HEAD_DIM, (G_NSA + g) * HEAD_DIM
        state = _flash_init(rows)
        d = qpos4 - (past - wb + lane_w)
        state = _flash_tile(q4, st_ref[0, pl.ds(g, wb, stride=slots), :].astype(BF16),
                            st_ref[0, pl.ds(G_NSA + g, wb, stride=slots), :].astype(BF16),
                            (d >= 0) & (d < WINDOW), state)
        d = qpos4 - (past + lane_n)
        state = _flash_tile(q4, new_ref[0, :, kcol:kcol + HEAD_DIM].astype(BF16),
                            new_ref[0, :, vcol:vcol + HEAD_DIM].astype(BF16),
                            (d >= 0) & (d < WINDOW) & (lane_n < nq), state)
        o_win = _flash_out(state)
        for n in range(HPG):
            h = g * HPG + n
            ocmp_ref[0, :, h * HEAD_DIM:(h + 1) * HEAD_DIM] = o_cmp[n * nq:(n + 1) * nq]
            owin_ref[0, :, h * HEAD_DIM:(h + 1) * HEAD_DIM] = o_win[n * nq:(n + 1) * nq]


def nsa_sample_select(q_nsa, kvc, state_win, win_new, past):
    bsz, nq = q_nsa.shape[:2]
    nbp = kvc.shape[3]
    nb = -(-(past + nq) // CMP_BLK)
    wrows = state_win.shape[1]
    return pl.pallas_call(
        functools.partial(_nsa_sample_sel_kernel, nq=nq, nbp=nbp, past=past, topk=min(TOP_BLOCKS, nb)),
        out_shape=[jax.ShapeDtypeStruct((bsz, nq, NSA_W), F32),
                   jax.ShapeDtypeStruct((bsz, nq, NSA_W), F32),
                   jax.ShapeDtypeStruct((bsz, G_NSA, nq, nbp * CMP_BLK), F32)],
        grid=(bsz,),
        in_specs=[pl.BlockSpec((1, nq, NSA_W), lambda b: (b, 0, 0)),
                  pl.BlockSpec((1, 1, G_NSA, nbp, HEAD_DIM), lambda b: (b, 0, 0, 0, 0)),
                  pl.BlockSpec((1, 1, G_NSA, nbp, HEAD_DIM), lambda b: (b, 1, 0, 0, 0)),
                  pl.BlockSpec((1, wrows, HEAD_DIM), lambda b: (b, 0, 0)),
                  pl.BlockSpec((1,) + win_new.shape[1:], lambda b: (b, 0, 0))],
        out_specs=[pl.BlockSpec((1, nq, NSA_W), lambda b: (b, 0, 0)),
                   pl.BlockSpec((1, nq, NSA_W), lambda b: (b, 0, 0)),
                   pl.BlockSpec((1, G_NSA, nq, nbp * CMP_BLK), lambda b: (b, 0, 0, 0))],
        compiler_params=_cparams(("arbitrary",)),
        name="nsa_sample_select",
    )(q_nsa, kvc, kvc, state_win, win_new)


def _nsa_sample_slc_kernel(pt_ref, *refs, pages, nq, past):
    page_refs = refs[:pages]
    (q_ref, new_ref, selx_ref, selx_new_ref, ocmp_ref, owin_ref, gate_ref, o_ref,
     kbuf, vbuf, m_scr, l_scr, acc_scr) = refs[pages:]
    jb = pl.program_id(1)
    rows = HPG * nq
    slots = 2 * G_NSA
    page = page_refs[0].shape[1] // slots
    t_in = lax.broadcasted_iota(jnp.int32, (nq, 1), 0)
    qpos4 = jnp.concatenate([past + t_in] * HPG, axis=0)
    q4s = [jnp.concatenate([q_ref[0, :, (g * HPG + n) * HEAD_DIM:(g * HPG + n + 1) * HEAD_DIM]
                            for n in range(HPG)], axis=0) for g in range(G_NSA)]

    @pl.when(jb == 0)
    def _():
        m_scr[...] = jnp.full(m_scr.shape, NEG_INF, F32)
        l_scr[...] = jnp.zeros_like(l_scr)
        acc_scr[...] = jnp.zeros_like(acc_scr)

    def sweep(g, k, v, sx, causal):
        mask = jnp.concatenate([sx] * HPG, axis=0) > 0.5
        if causal is not None:
            mask = mask & causal
        state = (m_scr[g], l_scr[g], acc_scr[g])
        m_scr[g], l_scr[g], acc_scr[g] = _flash_tile(q4s[g], k, v, mask, state)

    for g in range(G_NSA):
        for i, pr in enumerate(page_refs):
            dst = slice(i * page, (i + 1) * page)
            kbuf[g, dst, :] = pr[0, pl.ds(g, page, stride=slots), :].astype(BF16)
            vbuf[g, dst, :] = pr[0, pl.ds(G_NSA + g, page, stride=slots), :].astype(BF16)
        sweep(g, kbuf[g], vbuf[g], selx_ref[0, g], None)

    @pl.when(jb == pl.num_programs(1) - 1)
    def _():
        nrow = new_ref.shape[1]
        lane_n = lax.broadcasted_iota(jnp.int32, (rows, nrow), 1)
        for g in range(G_NSA):
            kcol, vcol = g * HEAD_DIM, (G_NSA + g) * HEAD_DIM
            sweep(g, new_ref[0, :, kcol:kcol + HEAD_DIM].astype(BF16),
                  new_ref[0, :, vcol:vcol + HEAD_DIM].astype(BF16),
                  selx_new_ref[0, g, :, 0:nrow], ((past + lane_n) <= qpos4) & (lane_n < nq))
        gates = gate_ref[0]
        for g in range(G_NSA):
            o_slc = _flash_out((m_scr[g], l_scr[g], acc_scr[g]))
            for n in range(HPG):
                h = g * HPG + n
                cols = slice(h * HEAD_DIM, (h + 1) * HEAD_DIM)
                o_ref[0, :, cols] = (gates[:, 3 * h:3 * h + 1] * ocmp_ref[0, :, cols]
                                     + gates[:, 3 * h + 1:3 * h + 2] * o_slc[n * nq:(n + 1) * nq]
                                     + gates[:, 3 * h + 2:3 * h + 3] * owin_ref[0, :, cols])


def nsa_sample_slc(cache_slc, page_table, q_nsa, slc_new, selx, o_cmp, o_win, gates):
    n_pool, page = cache_slc.shape[:2]
    bsz, n_pages = page_table.shape
    nq = q_nsa.shape[1]
    past = n_pages * page
    pages = min(PAGES_PER_STEP_NSA, n_pages)
    prow = page * 2 * G_NSA
    cache = cache_slc.reshape(n_pool, prow, HEAD_DIM)
    rows = HPG * nq

    def page_map(i):
        return lambda b, jb, pt: (pt[b, jb * pages + i], 0, 0)

    per_b = lambda b, jb, pt: (b, 0, 0)
    grid_spec = pltpu.PrefetchScalarGridSpec(
        num_scalar_prefetch=1,
        grid=(bsz, n_pages // pages),
        in_specs=[pl.BlockSpec((1, prow, HEAD_DIM), page_map(i)) for i in range(pages)]
        + [pl.BlockSpec((1, nq, NSA_W), per_b),
           pl.BlockSpec((1,) + slc_new.shape[1:], per_b),
           pl.BlockSpec((1, G_NSA, nq, pages * page), lambda b, jb, pt: (b, 0, 0, jb)),
           pl.BlockSpec((1, G_NSA, nq, LANES), lambda b, jb, pt: (b, 0, 0, past // LANES)),
           pl.BlockSpec((1, nq, NSA_W), per_b),
           pl.BlockSpec((1, nq, NSA_W), per_b),
           pl.BlockSpec((1, nq, GATE_W), per_b)],
        out_specs=pl.BlockSpec((1, nq, NSA_W), per_b),
        scratch_shapes=[pltpu.VMEM((G_NSA, pages * page, HEAD_DIM), BF16),
                        pltpu.VMEM((G_NSA, pages * page, HEAD_DIM), BF16),
                        pltpu.VMEM((G_NSA, rows, 1), F32), pltpu.VMEM((G_NSA, rows, 1), F32),
                        pltpu.VMEM((G_NSA, rows, HEAD_DIM), F32)],
    )
    return pl.pallas_call(
        functools.partial(_nsa_sample_slc_kernel, pages=pages, nq=nq, past=past),
        out_shape=jax.ShapeDtypeStruct((bsz, nq, NSA_W), F32),
        grid_spec=grid_spec,
        compiler_params=_cparams(("arbitrary", "arbitrary")),
        name="nsa_sample_slc",
    )(page_table, *([cache] * pages), q_nsa, slc_new, selx, selx, o_cmp, o_win, gates)


def _outproj_kernel(osb_ref, onsa_ref, gsb_ref, gnsa_ref, wa_ref, wb_ref, x_ref, gate_ref,
                    g2_ref, shift_ref, scale_ref, x1_ref, h2_ref):
    a = _rms(osb_ref[...], gsb_ref[...]).astype(BF16)
    b = _rms(onsa_ref[...], gnsa_ref[...]).astype(BF16)
    mixed = _dot(a, wa_ref[...]) + _dot(b, wb_ref[...])
    x1 = x_ref[...] + gate_ref[0] * mixed
    x1_ref[...] = x1
    h2_ref[...] = (_rms(x1, g2_ref[...]) * (1.0 + scale_ref[0]) + shift_ref[0]).astype(BF16)


def out_projection(o_sb, o_nsa, g_sb, g_nsa, w_out_bf, x2d, gate, g2, shift, scale, tm, mod_map):
    r, d = x2d.shape
    mod_block = (1,) + gate.shape[1:]
    mod_spec = pl.BlockSpec(mod_block, lambda i: (mod_map(i), 0, 0))
    row = lambda w: pl.BlockSpec((tm, w), lambda i: (i, 0))
    const = lambda shp: pl.BlockSpec(shp, lambda i: (0,) * len(shp))
    return pl.pallas_call(
        _outproj_kernel,
        out_shape=[jax.ShapeDtypeStruct((r, d), F32), jax.ShapeDtypeStruct((r, d), BF16)],
        grid=(r // tm,),
        in_specs=[row(SB_W), row(NSA_W), const((1, SB_W)), const((1, NSA_W)),
                  pl.BlockSpec((SB_W, d), lambda i: (0, 0)),
                  pl.BlockSpec((NSA_W, d), lambda i: (1, 0)),
                  row(d), mod_spec, const((1, d)), mod_spec, mod_spec],
        out_specs=[row(d), row(d)],
        compiler_params=_cparams(("arbitrary",)),
        name="out_projection",
    )(o_sb, o_nsa, g_sb, g_nsa, w_out_bf, w_out_bf, x2d, gate, g2, shift, scale)


def _peer_score_kernel(h_ref, wq_ref, k1_ref, k2_ref, s_ref):
    q = _dot(h_ref[...], wq_ref[...]).astype(BF16)
    half = D_QUERY // 2
    nchunk = h_ref.shape[0] // LANES
    for h in range(PEER_HEADS):
        for c, k_ref in enumerate((k1_ref, k2_ref)):
            qh = q[:, h * D_QUERY + c * half:h * D_QUERY + (c + 1) * half]
            st = _dot_nt(k_ref[h], qh)
            for ch in range(nchunk):
                s_ref[c, h, ch] = st[:, ch * LANES:(ch + 1) * LANES]


def peer_scores(h2, wq_bf, k1_bf, k2_bf):
    r, d = h2.shape
    tm = PEER_TN
    nchunk = tm // LANES
    return pl.pallas_call(
        _peer_score_kernel,
        out_shape=jax.ShapeDtypeStruct((2, PEER_HEADS, r // LANES, N_KEYS, LANES), F32),
        grid=(r // tm,),
        in_specs=[pl.BlockSpec((tm, d), lambda i: (i, 0)),
                  pl.BlockSpec(wq_bf.shape, lambda i: (0, 0)),
                  pl.BlockSpec(k1_bf.shape, lambda i: (0, 0, 0)),
                  pl.BlockSpec(k2_bf.shape, lambda i: (0, 0, 0))],
        out_specs=pl.BlockSpec((2, PEER_HEADS, nchunk, N_KEYS, LANES), lambda i: (0, 0, i, 0, 0)),
        compiler_params=_cparams(("arbitrary",)),
        name="peer_scores",
    )(h2, wq_bf, k1_bf, k2_bf)


def _top_values(x, n, scr):
    rank = jnp.full(x.shape, float(n), F32)
    for it in range(n):
        m = jnp.max(x, axis=0, keepdims=True)
        scr[it:it + 1, :] = m
        hit = x == m
        rank = jnp.where(hit, float(it), rank)
        x = jnp.where(hit, NEG_INF, x)
    return scr[0:n, :], rank


def _pack_bf16(x):
    return pltpu.bitcast(x.astype(BF16), jnp.int32)


def _peer_select_kernel(s_ref, r2_ref, e2_ref, cnt_ref, e1_ref, v1_scr, v2_scr):
    k = PEER_TOPK
    nchunk = s_ref.shape[2]

    def body(idx, _):
        h, ch = idx // nchunk, idx % nchunk
        s1 = s_ref[0, h, ch]
        s2 = s_ref[1, h, ch]
        v1, r1 = _top_values(s1, k, v1_scr)
        v2, r2 = _top_values(s2, k, v2_scr)
        x = jnp.concatenate([v1[0:1] + v2] + [v1[i:i + 1] + v2[0:k // 2] for i in range(1, k // 2)]
                            + [v1[k // 2:k] + v2[0:1]], axis=0)
        xs = x
        for _ in range(k - 1):
            xs = jnp.where(xs == jnp.max(xs, axis=0, keepdims=True), NEG_INF, xs)
        tau = jnp.max(xs, axis=0, keepdims=True)
        vmax = v1[0:1] + v2[0:1]
        chosen = x >= tau
        z = jnp.sum(jnp.where(chosen, jnp.exp(x - vmax), 0.0), axis=0, keepdims=True)
        c = jnp.where(chosen, 1.0, 0.0)
        cnt_rows = ([jnp.sum(c[0:k], axis=0, keepdims=True)]
                    + [jnp.sum(c[k + (k // 2) * (i - 1):k + (k // 2) * i], axis=0, keepdims=True)
                       for i in range(1, k // 2)])
        cnt_tab = jnp.concatenate(cnt_rows + [c[k + (k // 2) * (k // 2 - 1):]], axis=0)
        cnt = jnp.zeros_like(s1)
        for i in range(k):
            cnt = jnp.where(r1 == float(i), cnt_tab[i:i + 1], cnt)
        r2_ref[h, ch] = _pack_bf16(r2)
        e2_ref[h, ch] = _pack_bf16(jnp.exp(s2 - v2[0:1]))
        cnt_ref[h, ch] = cnt
        e1_ref[h, ch] = jnp.exp(s1 - v1[0:1]) / z
        return 0

    lax.fori_loop(0, PEER_HEADS * nchunk, body, 0)


def peer_select(s):
    _, heads, nch_all, keys, lanes = s.shape
    nchunk = PEER_TN // LANES
    blk = (heads, nchunk, keys, lanes)
    pblk = (heads, nchunk, keys // 2, lanes)
    full = lambda shp: (shp[0], nch_all) + shp[2:]
    return pl.pallas_call(
        _peer_select_kernel,
        out_shape=[jax.ShapeDtypeStruct(full(pblk), jnp.int32),
                   jax.ShapeDtypeStruct(full(pblk), jnp.int32),
                   jax.ShapeDtypeStruct(full(blk), F32),
                   jax.ShapeDtypeStruct(full(blk), F32)],
        grid=(nch_all // nchunk,),
        in_specs=[pl.BlockSpec((2,) + blk, lambda i: (0, 0, i, 0, 0))],
        out_specs=[pl.BlockSpec(pblk, lambda i: (0, i, 0, 0)),
                   pl.BlockSpec(pblk, lambda i: (0, i, 0, 0)),
                   pl.BlockSpec(blk, lambda i: (0, i, 0, 0)),
                   pl.BlockSpec(blk, lambda i: (0, i, 0, 0))],
        scratch_shapes=[pltpu.VMEM((PEER_TOPK, LANES), F32), pltpu.VMEM((PEER_TOPK, LANES), F32)],
        compiler_params=_cparams(("arbitrary",)),
        name="peer_select",
    )(s)


def _row_bf16(row):
    packed_rows = 2 * SUBLANES
    return pltpu.repeat(jnp.broadcast_to(row, (packed_rows, LANES)).astype(BF16), N_KEYS // packed_rows, axis=0)


def _peer_expert_kernel(h_ref, u_ref, vt_ref, r2_ref, e2_ref, cnt_ref, e1_ref, o_ref, ga0, ga1, g0, g1):
    e = pl.program_id(1)
    ne = pl.num_programs(1) - 2
    nchunk = h_ref.shape[0] // LANES

    @pl.when(e == 0)
    def _():
        o_ref[...] = jnp.zeros_like(o_ref)
        ga0[...] = jnp.zeros_like(ga0)
        g1[...] = jnp.zeros_like(g1)

    def step(g_cur, g_prev, ga_prev, ga_prev2):
        et = jnp.minimum(e, ne - 1)
        zero = jnp.zeros((N_KEYS, LANES), BF16)
        for ch in range(nchunk):
            for al in range(PEER_AT):
                a = et * PEER_AT + al
                gsum = zero
                for h in range(PEER_HEADS):
                    hit = pltpu.bitcast(r2_ref[h, ch], BF16) < _row_bf16(cnt_ref[h, ch, pl.ds(a, 1), :])
                    e2 = pltpu.bitcast(e2_ref[h, ch], BF16)
                    gsum = gsum + jnp.where(hit, e2, zero) * _row_bf16(e1_ref[h, ch, pl.ds(a, 1), :])
                g_cur[al * N_KEYS:(al + 1) * N_KEYS, ch * LANES:(ch + 1) * LANES] = gsum
        o_ref[...] += _dot(vt_ref[...], ga_prev2[...])
        act = _gelu(_dot_nt(u_ref[...], h_ref[...]))
        ga_prev[...] = act.astype(BF16) * g_prev[...]

    @pl.when(e % 2 == 0)
    def _():
        step(g0, g1, ga1, ga0)

    @pl.when(e % 2 == 1)
    def _():
        step(g1, g0, ga0, ga1)


def peer_experts(h2, u_bf, vt_bf, r2p, e2p, cnt, e1):
    r, d = h2.shape
    n_exp = u_bf.shape[0]
    tn = PEER_TN
    nchunk = tn // LANES
    et = PEER_AT * N_KEYS
    sblk = (PEER_HEADS, nchunk, N_KEYS, LANES)
    pblk = (PEER_HEADS, nchunk, N_KEYS // 2, LANES)
    ne = n_exp // et
    return pl.pallas_call(
        _peer_expert_kernel,
        out_shape=jax.ShapeDtypeStruct((d, r), F32),
        grid=(r // tn, ne + 2),
        in_specs=[pl.BlockSpec((tn, d), lambda t, e: (t, 0)),
                  pl.BlockSpec((et, d), lambda t, e: (jnp.clip(e - 1, 0, ne - 1), 0)),
                  pl.BlockSpec((d, et), lambda t, e: (0, jnp.clip(e - 2, 0, ne - 1))),
                  pl.BlockSpec(pblk, lambda t, e: (0, t, 0, 0)),
                  pl.BlockSpec(pblk, lambda t, e: (0, t, 0, 0)),
                  pl.BlockSpec(sblk, lambda t, e: (0, t, 0, 0)),
                  pl.BlockSpec(sblk, lambda t, e: (0, t, 0, 0))],
        out_specs=pl.BlockSpec((d, tn), lambda t, e: (0, t)),
        scratch_shapes=[pltpu.VMEM((et, tn), BF16), pltpu.VMEM((et, tn), BF16),
                        pltpu.VMEM((et, tn), BF16), pltpu.VMEM((et, tn), BF16)],
        compiler_params=_cparams(("arbitrary", "arbitrary")),
        name="peer_experts",
    )(h2, u_bf, vt_bf, r2p, e2p, cnt, e1)


def _final_kernel(x1_ref, pt_ref, gate_ref, g_ref, y_ref):
    x2 = x1_ref[...] + gate_ref[0] * pt_ref[...].T
    y_ref[...] = _rms(x2, g_ref[...])


def final_norm(x1, peer_t, col0, gate, g, tm, mod_map):
    r, d = x1.shape
    mod_block = (1,) + gate.shape[1:]
    return pl.pallas_call(
        _final_kernel,
        out_shape=jax.ShapeDtypeStruct((r, d), F32),
        grid=(r // tm,),
        in_specs=[pl.BlockSpec((tm, d), lambda i: (i, 0)),
                  pl.BlockSpec((d, tm), lambda i: (0, col0 // tm + i)),
                  pl.BlockSpec(mod_block, lambda i: (mod_map(i), 0, 0)),
                  pl.BlockSpec((1, d), lambda i: (0, 0))],
        out_specs=pl.BlockSpec((tm, d), lambda i: (i, 0)),
        compiler_params=_cparams(("arbitrary",)),
        name="final_norm",
    )(x1, peer_t, gate, g)


def _layer(x_prompt, x_sample, cache_sb, cache_cmp, cache_slc, state_win, page_table, c_prompt,
           c_sample, w_ada, b_ada, norm1_g, w_in, cmp_pe_k, cmp_w1_k, cmp_w2_k, cmp_pe_v, cmp_w1_v,
           cmp_w2_v, out_g_sb, out_g_nsa, w_out, norm2_g, peer_wq, peer_k1, peer_k2, peer_u, peer_v):
    b, t, d = x_prompt.shape
    bs, nq, _ = x_sample.shape
    n_pages, page = page_table.shape[1], cache_sb.shape[1]
    past = n_pages * page
    rp, rs = b * t, bs * nq

    w_in_bf = jnp.concatenate([w_in.astype(BF16), jnp.zeros((d, IN_NT * IN_TN - D_IN), BF16)], axis=1)
    w_out_bf = w_out.astype(BF16)
    wq_bf = peer_wq.astype(BF16)
    k1_bf, k2_bf = peer_k1.astype(BF16), peer_k2.astype(BF16)
    u_bf = peer_u.astype(BF16)
    vt_bf = peer_v.astype(BF16).T
    pe = jnp.stack([cmp_pe_k, cmp_pe_v])
    w1 = jnp.stack([cmp_w1_k, cmp_w1_v]).astype(BF16).reshape(2, CMP_BLK * HEAD_DIM, HEAD_DIM)
    w2 = jnp.stack([cmp_w2_k, cmp_w2_v]).astype(BF16)
    row = lambda v: v.reshape(1, -1)

    n_c = b + bs
    c_all = jnp.pad(jnp.concatenate([c_prompt, c_sample]), ((0, -n_c % 16), (0, 0))).astype(BF16)
    mod = adaln_mod(c_all, w_ada, row(b_ada))
    mod_p = [mod[:b, k * d:(k + 1) * d].reshape(b, 1, d) for k in range(6)]
    mod_s = [jnp.repeat(mod[b:n_c, k * d:(k + 1) * d], nq, axis=0).reshape(1, rs, d) for k in range(6)]

    tm_p = 512 if t % 512 == 0 else t
    xp = x_prompt.reshape(rp, d)
    tabs_p = _rope_tables(jnp.arange(t))
    qsb_p, kvsb_p, qnsa_p, kvcmp_p, kvslc_p, kvwin_p, gates_p = in_projection(
        xp, row(norm1_g), mod_p[0], mod_p[1], w_in_bf, tabs_p, tm_p,
        lambda i: i // (t // tm_p), lambda i: i % (t // tm_p))
    osb_p = sb_prompt(qsb_p, kvsb_p, b, t)
    kvc_p = compress_prompt(kvcmp_p, pe, w1, w2, b, t)
    onsa_p = nsa_prompt(qnsa_p, kvc_p, kvslc_p, kvwin_p, gates_p, b, t)
    tm_o = 256
    x1_p, h2_p = out_projection(osb_p, onsa_p, row(out_g_sb), row(out_g_nsa), w_out_bf, xp,
                                mod_p[2], row(norm2_g), mod_p[3], mod_p[4], tm_o,
                                lambda i: i // (t // tm_o))

    xs = x_sample.reshape(rs, d)
    tabs_s = _rope_tables(past + jnp.tile(jnp.arange(nq), bs))
    qsb_s, kvsb_s, qnsa_s, kvcmp_s, kvslc_s, kvwin_s, gates_s = in_projection(
        xs, row(norm1_g), mod_s[0], mod_s[1], w_in_bf, tabs_s, rs, lambda i: 0, lambda i: 0)
    osb_s = sb_sample(cache_sb, page_table, qsb_s.reshape(bs, nq, SB_W), kvsb_s.reshape(bs, nq, 2 * SB_W))
    kvcmp_s3 = kvcmp_s.reshape(bs, nq, KV_W)
    kvc_s = compress_sample(cache_cmp, page_table, kvcmp_s3, pe, w1, w2)
    pad8 = lambda a: jnp.pad(a.reshape(bs, nq, KV_W), ((0, 0), (0, -nq % SUBLANES), (0, 0)))
    qnsa_s3 = qnsa_s.reshape(bs, nq, NSA_W)
    st_win = state_win.reshape(bs, state_win.shape[1] * 2 * G_NSA, HEAD_DIM)
    ocmp_s, owin_s, selx = nsa_sample_select(qnsa_s3, kvc_s, st_win, pad8(kvwin_s), past)
    onsa_s = nsa_sample_slc(cache_slc, page_table, qnsa_s3, pad8(kvslc_s), selx, ocmp_s, owin_s,
                            gates_s.reshape(bs, nq, GATE_W))
    x1_s, h2_s = out_projection(osb_s.reshape(rs, SB_W), onsa_s.reshape(rs, NSA_W), row(out_g_sb),
                                row(out_g_nsa), w_out_bf, xs, mod_s[2], row(norm2_g), mod_s[3],
                                mod_s[4], rs, lambda i: 0)

    r = rp + rs
    r_pad = -(-r // PEER_TN) * PEER_TN
    h2 = jnp.pad(jnp.concatenate([h2_p, h2_s]), ((0, r_pad - r), (0, 0)))
    s = peer_scores(h2, wq_bf, k1_bf, k2_bf)
    r2p, e2p, cnt, e1 = peer_select(s)
    peer_t = peer_experts(h2, u_bf, vt_bf, r2p, e2p, cnt, e1)

    states_p = (kvsb_p.reshape(b, t, 2, H_SB, HEAD_DIM), kvcmp_p.reshape(b, t, 2, G_NSA, HEAD_DIM),
                kvslc_p.reshape(b, t, 2, G_NSA, HEAD_DIM),
                kvwin_p.reshape(b, t, 2, G_NSA, HEAD_DIM)[:, t - min(WINDOW, t):])
    kvwin_s5 = kvwin_s.reshape(bs, nq, 2, G_NSA, HEAD_DIM)
    states_s = (kvsb_s.reshape(bs, nq, 2, H_SB, HEAD_DIM), kvcmp_s.reshape(bs, nq, 2, G_NSA, HEAD_DIM),
                kvslc_s.reshape(bs, nq, 2, G_NSA, HEAD_DIM),
                jnp.concatenate([state_win, kvwin_s5], axis=1)[:, nq:])
    return (x1_p, mod_p[5], x1_s, mod_s[5], peer_t, rp), states_p, states_s


def kernel(x_prompt, x_sample, cache_sb, cache_cmp, cache_slc, state_win, page_table, c_prompt, c_sample, w_ada, b_ada, norm1_g, w_in, cmp_pe_k, cmp_w1_k, cmp_w2_k, cmp_pe_v, cmp_w1_v, cmp_w2_v, out_g_sb, out_g_nsa, w_out, norm2_g, peer_wq, peer_k1, peer_k2, peer_u, peer_v, final_g):
    depth = w_ada.shape[0]
    assert depth == 1, "single-layer trunk"
    b, t, d = x_prompt.shape
    bs, nq, _ = x_sample.shape
    (x1_p, gate_p, x1_s, gate_s, peer_t, rp), st_p, st_s = _layer(
        x_prompt, x_sample, cache_sb[0], cache_cmp[0], cache_slc[0], state_win[0], page_table,
        c_prompt, c_sample, w_ada[0], b_ada[0], norm1_g[0], w_in[0], cmp_pe_k[0], cmp_w1_k[0],
        cmp_w2_k[0], cmp_pe_v[0], cmp_w1_v[0], cmp_w2_v[0], out_g_sb[0], out_g_nsa[0], w_out[0],
        norm2_g[0], peer_wq[0], peer_k1[0], peer_k2[0], peer_u[0], peer_v[0])
    fg = final_g.reshape(1, d)
    tm_f = 256
    y_p = final_norm(x1_p, peer_t, 0, gate_p, fg, tm_f, lambda i: i // (t // tm_f)).reshape(b, t, d)
    rs = bs * nq
    y_s = final_norm(x1_s, peer_t, rp, gate_s, fg, rs, lambda i: 0).reshape(bs, nq, d)
    return (y_p, y_s, st_p[0][None], st_s[0][None], st_p[1][None], st_s[1][None],
            st_p[2][None], st_s[2][None], st_p[3][None], st_s[3][None])
```

```python
import functools

import jax
import jax.numpy as jnp
from jax import lax
from jax.experimental import pallas as pl
from jax.experimental.pallas import tpu as pltpu

F32 = jnp.float32
BF16 = jnp.bfloat16

HEAD_DIM = 128
H_SB = 8
H_NSA = 8
G_NSA = 2
HPG = H_NSA // G_NSA
SB_W = H_SB * HEAD_DIM
NSA_W = H_NSA * HEAD_DIM
KV_W = 2 * G_NSA * HEAD_DIM
D_IN = 3 * SB_W + NSA_W + 3 * KV_W + 3 * H_NSA
SCALE = HEAD_DIM ** -0.5
ROT_HALF = HEAD_DIM // 8
ROPE_THETA = 500000.0
CMP_BLK = 64
TOP_BLOCKS = 16
WINDOW = 512
FORCE_SCORE = 1e9
N_KEYS = 128
PEER_HEADS = 8
PEER_TOPK = 16
D_QUERY = 256
EPS = 1e-6
NEG_INF = -1e30

LANES = 128
SUBLANES = 8
VMEM_LIMIT = 56 * 1024 * 1024

MXU_N = 256
IN_TN = 2 * MXU_N
IN_NT = -(-D_IN // IN_TN)
GATE_W = LANES
PEER_TN = 3 * MXU_N
CMP_PITCH = CMP_BLK * 2 * G_NSA + SUBLANES
PEER_AT = 4
PAGES_PER_STEP_SB = 8
PAGES_PER_STEP_NSA = 16
PAGES_PER_STEP_SLC = 32


def _cparams(sem):
    return pltpu.CompilerParams(dimension_semantics=sem, vmem_limit_bytes=VMEM_LIMIT)


def _dot(a, b):
    return jnp.dot(a, b, preferred_element_type=F32)


def _dot_nt(a, b):
    return lax.dot_general(a, b, (((1,), (1,)), ((), ())), preferred_element_type=F32)


def _dot_tn(a, b):
    return lax.dot_general(a, b, (((0,), (0,)), ((), ())), preferred_element_type=F32)


def _gelu(x):
    return 0.5 * x * (1.0 + jnp.tanh(0.7978845608028654 * (x + 0.044715 * (x * x * x))))


def _rms(x, g):
    return x * lax.rsqrt(jnp.mean(x * x, axis=-1, keepdims=True) + EPS) * g


def _softplus(z):
    return jnp.maximum(z, 0.0) + jnp.log(1.0 + jnp.exp(-jnp.abs(z)))


def _split_bf16(x):
    hi = x.astype(BF16)
    lo = (x - hi.astype(F32)).astype(BF16)
    return hi, lo


def _mod_kernel(c_ref, w_ref, b_ref, o_ref):
    o_ref[...] = _dot(c_ref[...], w_ref[...].astype(BF16)) + b_ref[...]


def adaln_mod(c_bf, w_ada, b_ada):
    m, d = c_bf.shape
    n = w_ada.shape[1]
    tn = 1024
    return pl.pallas_call(
        _mod_kernel,
        out_shape=jax.ShapeDtypeStruct((m, n), F32),
        grid=(n // tn,),
        in_specs=[pl.BlockSpec((m, d), lambda j: (0, 0)),
                  pl.BlockSpec((d, tn), lambda j: (0, j)),
                  pl.BlockSpec((1, tn), lambda j: (0, j))],
        out_specs=pl.BlockSpec((m, tn), lambda j: (0, j)),
        compiler_params=_cparams(("arbitrary",)),
        name="adaln_mod",
    )(c_bf, w_ada, b_ada)


_IN_SEGMENTS = (
    (0, 2, 0, "plain"),
    (2, 6, 1, "plain"),
    (6, 8, 2, "rope"),
    (8, 9, 3, "rope_k"),
    (9, 10, 4, "rope_k"),
    (10, 11, 5, "rope_k"),
    (11, 12, 6, "gate"),
)
_IN_OUT_RANGES = ((0, 2), (2, 6), (6, 8), (8, 9), (9, 10), (10, 11), (11, 12))


def _rope(p, rc, ra, rb):
    w = p.shape[1]
    return p * rc + pltpu.roll(p, ROT_HALF, 1) * ra + pltpu.roll(p, w - ROT_HALF, 1) * rb


def _inproj_kernel(x_ref, g_ref, shift_ref, scale_ref, w_ref, rc_ref, ra_ref, rb_ref,
                   *rest):
    outs, h_scr = rest[:7], rest[7]
    j = pl.program_id(1)

    @pl.when(j == 0)
    def _():
        h = _rms(x_ref[...], g_ref[...]) * (1.0 + scale_ref[0]) + shift_ref[0]
        h_scr[...] = h.astype(BF16)

    p = _dot(h_scr[...], w_ref[...])
    kw = G_NSA * HEAD_DIM
    for lo, hi, slot, kind in _IN_SEGMENTS:
        @pl.when((j >= lo) & (j < hi))
        def _(slot=slot, kind=kind):
            if kind == "rope":
                v = _rope(p, rc_ref[...], ra_ref[...], rb_ref[...])
            elif kind == "rope_k":
                kk = _rope(p[:, :kw], rc_ref[:, :kw], ra_ref[:, :kw], rb_ref[:, :kw])
                nslot = 2 * G_NSA
                for s in range(nslot):
                    src = kk if s < G_NSA else p
                    outs[slot][pl.ds(s, p.shape[0], stride=nslot), :] = src[:, s * HEAD_DIM:(s + 1) * HEAD_DIM]
                return
            elif kind == "gate":
                v = jax.nn.sigmoid(p[:, :GATE_W])
            else:
                v = p
            outs[slot][...] = v.astype(outs[slot].dtype)


def _rope_tables(pos):
    inv = jnp.power(ROPE_THETA, -jnp.arange(ROT_HALF, dtype=F32) / ROT_HALF)
    ang = pos.astype(F32)[:, None] * inv[None, :]
    cos, sin = jnp.cos(ang), jnp.sin(ang)
    n = pos.shape[0]
    z16 = jnp.zeros((n, ROT_HALF), F32)
    tail0 = jnp.zeros((n, HEAD_DIM - 2 * ROT_HALF), F32)
    rc = jnp.concatenate([cos, cos, tail0 + 1.0], axis=1)
    ra = jnp.concatenate([z16, sin, tail0], axis=1)
    rb = jnp.concatenate([-sin, z16, tail0], axis=1)
    heads = IN_TN // HEAD_DIM
    return tuple(jnp.tile(t, (1, heads)) for t in (rc, ra, rb))


def in_projection(x2d, g1, shift, scale, w_bf, tables, tm, mod_map, tab_map):
    r, d = x2d.shape
    nslot = 2 * G_NSA
    wide = lambda lo, hi: ((r, (hi - lo) * IN_TN), (tm, IN_TN))
    flat = ((r * nslot, HEAD_DIM), (tm * nslot, HEAD_DIM))
    shapes = ([wide(*_IN_OUT_RANGES[k]) for k in range(3)] + [flat] * 3 + [((r, GATE_W), (tm, GATE_W))])
    dtypes = [BF16, F32, BF16, F32, F32, F32, F32]
    out_shape = [jax.ShapeDtypeStruct(s, dt) for (s, _), dt in zip(shapes, dtypes)]

    def out_map(lo, hi):
        return lambda i, j: (i, jnp.clip(j - lo, 0, hi - lo - 1))

    mod_block = (1,) + shift.shape[1:]
    return pl.pallas_call(
        _inproj_kernel,
        out_shape=out_shape,
        grid=(r // tm, IN_NT),
        in_specs=[pl.BlockSpec((tm, d), lambda i, j: (i, 0)),
                  pl.BlockSpec((1, d), lambda i, j: (0, 0)),
                  pl.BlockSpec(mod_block, lambda i, j: (mod_map(i), 0, 0)),
                  pl.BlockSpec(mod_block, lambda i, j: (mod_map(i), 0, 0)),
                  pl.BlockSpec((d, IN_TN), lambda i, j: (0, j)),
                  pl.BlockSpec((tm, IN_TN), lambda i, j: (tab_map(i), 0)),
                  pl.BlockSpec((tm, IN_TN), lambda i, j: (tab_map(i), 0)),
                  pl.BlockSpec((tm, IN_TN), lambda i, j: (tab_map(i), 0))],
        out_specs=[pl.BlockSpec(blk, out_map(lo, hi))
                   for (_, blk), (lo, hi) in zip(shapes, _IN_OUT_RANGES)],
        scratch_shapes=[pltpu.VMEM((tm, d), BF16)],
        compiler_params=_cparams(("arbitrary", "arbitrary")),
        name="in_projection",
    )(x2d, g1, shift, scale, w_bf, *tables)


def _sb_weights(z, valid, later, tri):
    rows, keys = z.shape
    sub = tri.shape[0]
    nsub = keys // sub
    sp = _softplus(z)
    log1m = -sp if valid is None else jnp.where(valid, -sp, 0.0)
    stacked = jnp.concatenate([log1m[:, i * sub:(i + 1) * sub] for i in range(nsub)], axis=0)
    hi, lo = _split_bf16(stacked)
    within = _dot(hi, tri) + _dot(lo, tri)
    afters = [None] * nsub
    for i in reversed(range(nsub)):
        w = within[i * rows:(i + 1) * rows]
        afters[i] = w + later
        later = later + w[:, 0:1] + log1m[:, i * sub:i * sub + 1]
    a = jnp.exp(z - sp + jnp.concatenate(afters, axis=1))
    if valid is not None:
        a = jnp.where(valid, a, 0.0)
    return a, later


def _later_keys_tri(sub):
    return (lax.broadcasted_iota(jnp.int32, (sub, sub), 0)
            > lax.broadcasted_iota(jnp.int32, (sub, sub), 1)).astype(BF16)


def _sb_prompt_kernel(q_ref, k_ref, v_ref, o_ref, *, tq, tk, nh):
    qi = pl.program_id(2)
    qpos = qi * tq + lax.broadcasted_iota(jnp.int32, (tq, tk), 0)
    lane = lax.broadcasted_iota(jnp.int32, (tq, tk), 1)
    tri = _later_keys_tri(min(MXU_N, tk))
    nk = ((qi + 1) * tq + tk - 1) // tk
    heads = [slice(hh * HEAD_DIM, (hh + 1) * HEAD_DIM) for hh in range(nh)]
    qs = [q_ref[:, cols] for cols in heads]

    def body(it, carry, masked):
        start = pl.multiple_of((nk - 1 - it) * tk, tk)
        valid = (start + lane) < qpos if masked else None
        out = []
        for (later, acc), q, cols in zip(carry, qs, heads):
            k = k_ref[pl.ds(start, tk), cols].astype(BF16)
            v = v_ref[pl.ds(start, tk), cols].astype(BF16)
            a, later = _sb_weights(_dot_nt(q, k) * SCALE, valid, later, tri)
            out.append((later, acc + _dot(a.astype(BF16), v)))
        return tuple(out)

    nfull = (qi * tq) // tk
    init = tuple((jnp.zeros((tq, 1), F32), jnp.zeros((tq, HEAD_DIM), F32)) for _ in heads)
    carry = lax.fori_loop(0, nk - nfull, functools.partial(body, masked=True), init)
    carry = lax.fori_loop(nk - nfull, nk, functools.partial(body, masked=False), carry)
    for (_, acc), cols in zip(carry, heads):
        o_ref[:, cols] = acc


def sb_prompt(q_sb, kv_sb, b, t):
    tq = 256
    tk = 512 if t % 512 == 0 else tq
    nq = t // tq
    nh = 2
    hw = nh * HEAD_DIM
    return pl.pallas_call(
        functools.partial(_sb_prompt_kernel, tq=tq, tk=tk, nh=nh),
        out_shape=jax.ShapeDtypeStruct((b * t, SB_W), F32),
        grid=(b, H_SB // nh, nq),
        in_specs=[pl.BlockSpec((tq, hw), lambda bi, h, qi: (bi * nq + qi, h)),
                  pl.BlockSpec((t, hw), lambda bi, h, qi: (bi, h)),
                  pl.BlockSpec((t, hw), lambda bi, h, qi: (bi, H_SB // nh + h))],
        out_specs=pl.BlockSpec((tq, hw), lambda bi, h, qi: (bi * nq + qi, h)),
        compiler_params=_cparams(("arbitrary", "arbitrary", "arbitrary")),
        name="sb_prompt",
    )(q_sb, kv_sb, kv_sb)


def _compress_rows(read_rows, pe_ref, w1_ref, w2_ref, x_scr, c):
    for p in range(CMP_BLK):
        x = jnp.concatenate([read_rows(p, 0), read_rows(p, 1)], axis=0) + pe_ref[c, p:p + 1, :]
        x_scr[:, p * HEAD_DIM:(p + 1) * HEAD_DIM] = x.astype(BF16)
    hid = _gelu(_dot(x_scr[...], w1_ref[c]))
    return _dot(hid.astype(BF16), w2_ref[c])


def _cmp_prompt_kernel(x_ref, pe_ref, w1_ref, w2_ref, o_ref, x_scr, *, nb):
    stride = CMP_BLK * 2 * G_NSA
    for c in range(2):
        def read_rows(p, g, c=c):
            return x_ref[0, pl.ds(p * 2 * G_NSA + c * G_NSA + g, nb, stride=stride), :]
        out = _compress_rows(read_rows, pe_ref, w1_ref, w2_ref, x_scr, c)
        o_ref[0, c, 0] = out[:nb]
        o_ref[0, c, 1] = out[nb:]


def compress_prompt(kv_cmp, pe, w1, w2, b, t):
    nb = t // CMP_BLK
    x = kv_cmp.reshape(b, t * 2 * G_NSA, HEAD_DIM)
    return pl.pallas_call(
        functools.partial(_cmp_prompt_kernel, nb=nb),
        out_shape=jax.ShapeDtypeStruct((b, 2, G_NSA, nb, HEAD_DIM), F32),
        grid=(b,),
        in_specs=[pl.BlockSpec((1, t * 2 * G_NSA, HEAD_DIM), lambda i: (i, 0, 0)),
                  pl.BlockSpec(pe.shape, lambda i: (0, 0, 0)),
                  pl.BlockSpec(w1.shape, lambda i: (0, 0, 0)),
                  pl.BlockSpec(w2.shape, lambda i: (0, 0, 0))],
        out_specs=pl.BlockSpec((1, 2, G_NSA, nb, HEAD_DIM), lambda i: (i, 0, 0, 0, 0)),
        scratch_shapes=[pltpu.VMEM((G_NSA * nb, CMP_BLK * HEAD_DIM), BF16)],
        compiler_params=_cparams(("arbitrary",)),
        name="compress_prompt",
    )(x, pe, w1, w2)


def _cmp_sample_kernel(pt_ref, *refs, pages, nbp, past):
    page_refs = refs[:pages]
    new_ref, pe_ref, w1_ref, w2_ref, o_ref, buf, x_scr = refs[pages:]
    b, jb = pl.program_id(0), pl.program_id(1)
    blk_rows = CMP_BLK * 2 * G_NSA
    blks_per_page = page_refs[0].shape[1] // blk_rows
    past_blks = past // CMP_BLK

    @pl.when((b == 0) & (jb == 0))
    def _():
        tail = buf.shape[0] - past_blks * CMP_PITCH
        buf[pl.ds(past_blks * CMP_PITCH, tail), :] = jnp.zeros((tail, HEAD_DIM), F32)

    for i, pr in enumerate(page_refs):
        for hb in range(blks_per_page):
            blk = (jb * pages + i) * blks_per_page + hb
            start = pl.multiple_of(blk * CMP_PITCH, SUBLANES)
            buf[pl.ds(start, blk_rows), :] = pr[0, hb * blk_rows:(hb + 1) * blk_rows, :]

    @pl.when(jb == pl.num_programs(1) - 1)
    def _():
        buf[pl.ds(past_blks * CMP_PITCH, new_ref.shape[1]), :] = new_ref[0]
        for c in range(2):
            def read_rows(p, g, c=c):
                return buf[pl.ds(p * 2 * G_NSA + c * G_NSA + g, nbp, stride=CMP_PITCH), :]
            out = _compress_rows(read_rows, pe_ref, w1_ref, w2_ref, x_scr, c)
            o_ref[0, c, 0] = out[:nbp]
            o_ref[0, c, 1] = out[nbp:]


def compress_sample(cache_cmp, page_table, kv_new, pe, w1, w2):
    n_pool, page = cache_cmp.shape[:2]
    bsz, n_pages = page_table.shape
    nq = kv_new.shape[1]
    past = n_pages * page
    nb = -(-(past + nq) // CMP_BLK)
    nbp = -(-nb // SUBLANES) * SUBLANES
    pages = min(PAGES_PER_STEP_NSA, n_pages)
    rpp = page * 2 * G_NSA
    cache = cache_cmp.reshape(n_pool, rpp, HEAD_DIM)
    new = kv_new.reshape(bsz, nq * 2 * G_NSA, HEAD_DIM)

    def page_map(i):
        return lambda b, jb, pt: (pt[b, jb * pages + i], 0, 0)

    grid_spec = pltpu.PrefetchScalarGridSpec(
        num_scalar_prefetch=1,
        grid=(bsz, n_pages // pages),
        in_specs=[pl.BlockSpec((1, rpp, HEAD_DIM), page_map(i)) for i in range(pages)]
        + [pl.BlockSpec((1,) + new.shape[1:], lambda b, jb, pt: (b, 0, 0)),
           pl.BlockSpec(pe.shape, lambda b, jb, pt: (0, 0, 0)),
           pl.BlockSpec(w1.shape, lambda b, jb, pt: (0, 0, 0)),
           pl.BlockSpec(w2.shape, lambda b, jb, pt: (0, 0, 0))],
        out_specs=pl.BlockSpec((1, 2, G_NSA, nbp, HEAD_DIM), lambda b, jb, pt: (b, 0, 0, 0, 0)),
        scratch_shapes=[pltpu.VMEM((nbp * CMP_PITCH, HEAD_DIM), F32),
                        pltpu.VMEM((G_NSA * nbp, CMP_BLK * HEAD_DIM), BF16)],
    )
    return pl.pallas_call(
        functools.partial(_cmp_sample_kernel, pages=pages, nbp=nbp, past=past),
        out_shape=jax.ShapeDtypeStruct((bsz, 2, G_NSA, nbp, HEAD_DIM), F32),
        grid_spec=grid_spec,
        compiler_params=_cparams(("arbitrary", "arbitrary")),
        name="compress_sample",
    )(page_table, *([cache] * pages), new, pe, w1, w2)


def _cmp_branch(q4, qpos4, kc, vc, nq, nbp):
    blk = lax.broadcasted_iota(jnp.int32, (1, nbp), 1)
    s = _dot_nt(q4, kc.astype(BF16)) * SCALE
    complete = (blk + 1) * CMP_BLK <= qpos4 + 1
    m = jnp.max(jnp.where(complete, s, NEG_INF), axis=1, keepdims=True)
    e = jnp.where(complete, jnp.exp(s - m), 0.0)
    den = jnp.sum(e, axis=1, keepdims=True)
    p = e / jnp.where(den > 0.0, den, 1.0)
    o_cmp = _dot(p.astype(BF16), vc.astype(BF16))
    imp = p[0:nq]
    for n in range(1, HPG):
        imp = imp + p[n * nq:(n + 1) * nq]
    qpos = qpos4[0:nq]
    cur = qpos // CMP_BLK
    started = blk <= cur
    forced = started & ((blk == 0) | (blk >= cur - 1))
    score = jnp.where(forced, FORCE_SCORE, jnp.where(started, imp, -FORCE_SCORE))
    rank = jnp.zeros((nq, nbp), F32)
    for j in range(nbp):
        col = score[:, j:j + 1]
        rank = rank + jnp.where(col > score, 1.0, 0.0) + jnp.where((col == score) & (blk > j), 1.0, 0.0)
    return o_cmp, rank


def _cmp_branch_t(q4, qpos_row, kc, vc, nq, nb):
    blk = lax.broadcasted_iota(jnp.int32, (nb, 1), 0)
    s = _dot_nt(kc.astype(BF16), q4) * SCALE
    complete = (blk + 1) * CMP_BLK <= qpos_row + 1
    m = jnp.max(jnp.where(complete, s, NEG_INF), axis=0, keepdims=True)
    e = jnp.where(complete, jnp.exp(s - m), 0.0)
    den = jnp.sum(e, axis=0, keepdims=True)
    p = e / jnp.where(den > 0.0, den, 1.0)
    o_cmp = _dot_tn(p.astype(BF16), vc.astype(BF16))
    imp = p[:, 0:nq]
    for n in range(1, HPG):
        imp = imp + p[:, n * nq:(n + 1) * nq]
    cur = qpos_row[:, 0:nq] // CMP_BLK
    started = blk <= cur
    forced = started & ((blk == 0) | (blk >= cur - 1))
    score = jnp.where(forced, FORCE_SCORE, jnp.where(started, imp, -FORCE_SCORE))
    rank = jnp.zeros((nb, nq), F32)
    for j in range(nb):
        row = score[j:j + 1, :]
        rank = rank + jnp.where(row > score, 1.0, 0.0) + jnp.where((row == score) & (blk > j), 1.0, 0.0)
    return o_cmp, rank


def _flash_tile(q4, k, v, mask, state):
    m, l, acc = state
    s = _dot_nt(q4, k) * SCALE
    m_new = jnp.maximum(m, jnp.max(jnp.where(mask, s, NEG_INF), axis=1, keepdims=True))
    p = jnp.where(mask, jnp.exp(s - m_new), 0.0)
    alpha = jnp.exp(m - m_new)
    l = alpha * l + jnp.sum(p, axis=1, keepdims=True)
    acc = alpha * acc + _dot(p.astype(BF16), v)
    return m_new, l, acc


def _flash_init(rows):
    return (jnp.full((rows, 1), NEG_INF, F32), jnp.zeros((rows, 1), F32),
            jnp.zeros((rows, HEAD_DIM), F32))


def _flash_out(state):
    _, l, acc = state
    return acc / jnp.where(l > 0.0, l, 1.0)


def _stack_heads(q_ref):
    return jnp.concatenate([q_ref[:, n * HEAD_DIM:(n + 1) * HEAD_DIM] for n in range(HPG)], axis=0)


def _nsa_prompt_kernel(q_ref, kc_ref, vc_ref, slc_ref, win_ref, gate_ref, o_ref,
                       selx_scr, *, tq, tk, nb, topk):
    g, qi = pl.program_id(1), pl.program_id(2)
    rows = HPG * tq
    nslot = 2 * G_NSA
    t = slc_ref.shape[0] // nslot
    q4 = _stack_heads(q_ref)

    def keys(ref, start, n):
        return (ref[pl.ds(start * nslot + g, n, stride=nslot), :].astype(BF16),
                ref[pl.ds(start * nslot + G_NSA + g, n, stride=nslot), :].astype(BF16))
    t_in = lax.broadcasted_iota(jnp.int32, (tq, 1), 0)
    qpos4 = jnp.concatenate([qi * tq + t_in] * HPG, axis=0)
    qpos_row = qi * tq + lax.broadcasted_iota(jnp.int32, (1, rows), 1) % tq
    o_cmp, rank_t = _cmp_branch_t(q4, qpos_row, kc_ref[0, 0, 0], vc_ref[0, 0, 0], tq, nb)
    sel_t = jnp.where(rank_t < topk, 1.0, 0.0).astype(BF16)
    kb = lax.broadcasted_iota(jnp.int32, (nb, t), 0)
    kl = lax.broadcasted_iota(jnp.int32, (nb, t), 1)
    selx = _dot_tn(sel_t, jnp.where(kl // CMP_BLK == kb, 1.0, 0.0).astype(BF16))
    for j in range(t // tk):
        selx_scr[j] = selx[:, j * tk:(j + 1) * tk]
    lane = lax.broadcasted_iota(jnp.int32, (rows, tk), 1)

    def slc_body(j, state):
        start = pl.multiple_of(j * tk, tk)
        k, v = keys(slc_ref, start, tk)
        sx = selx_scr[j]
        mask = (jnp.concatenate([sx] * HPG, axis=0) > 0.5) & ((start + lane) <= qpos4)
        return _flash_tile(q4, k, v, mask, state)

    nk = ((qi + 1) * tq + tk - 1) // tk
    o_slc = _flash_out(lax.fori_loop(0, nk, slc_body, _flash_init(rows)))

    wlen = min(WINDOW + tq, t)
    start = pl.multiple_of(jnp.clip(qi * tq - WINDOW, 0, t - wlen), tq)
    d = qpos4 - (start + lax.broadcasted_iota(jnp.int32, (rows, wlen), 1))
    o_win = _flash_out(_flash_tile(q4, *keys(win_ref, start, wlen), (d >= 0) & (d < WINDOW),
                                   _flash_init(rows)))

    gates = gate_ref[...]

    def gate(n, branch):
        col = gates[:, n * 3 + branch:n * 3 + branch + 1]
        for gg in range(1, G_NSA):
            c = (gg * HPG + n) * 3 + branch
            col = jnp.where(g == gg, gates[:, c:c + 1], col)
        return col

    for n in range(HPG):
        r0 = n * tq
        o_ref[:, n * HEAD_DIM:(n + 1) * HEAD_DIM] = (
            gate(n, 0) * o_cmp[r0:r0 + tq] + gate(n, 1) * o_slc[r0:r0 + tq]
            + gate(n, 2) * o_win[r0:r0 + tq])


def nsa_prompt(q_nsa, kvc, kv_slc, kv_win, gates, b, t):
    tq = 128
    tk = 512 if t % 512 == 0 else tq
    nq = t // tq
    nb = t // CMP_BLK
    topk = min(TOP_BLOCKS, nb)
    gw = HPG * HEAD_DIM
    return pl.pallas_call(
        functools.partial(_nsa_prompt_kernel, tq=tq, tk=tk, nb=nb, topk=topk),
        out_shape=jax.ShapeDtypeStruct((b * t, NSA_W), F32),
        grid=(b, G_NSA, nq),
        in_specs=[pl.BlockSpec((tq, gw), lambda bi, g, qi: (bi * nq + qi, g)),
                  pl.BlockSpec((1, 1, 1, nb, HEAD_DIM), lambda bi, g, qi: (bi, 0, g, 0, 0)),
                  pl.BlockSpec((1, 1, 1, nb, HEAD_DIM), lambda bi, g, qi: (bi, 1, g, 0, 0)),
                  pl.BlockSpec((t * 2 * G_NSA, HEAD_DIM), lambda bi, g, qi: (bi, 0)),
                  pl.BlockSpec((t * 2 * G_NSA, HEAD_DIM), lambda bi, g, qi: (bi, 0)),
                  pl.BlockSpec((tq, GATE_W), lambda bi, g, qi: (bi * nq + qi, 0))],
        out_specs=pl.BlockSpec((tq, gw), lambda bi, g, qi: (bi * nq + qi, g)),
        scratch_shapes=[pltpu.VMEM((t // tk, tq, tk), F32)],
        compiler_params=_cparams(("arbitrary", "arbitrary", "arbitrary")),
        name="nsa_prompt",
    )(q_nsa, kvc, kvc, kv_slc, kv_win, gates)


def _sb_sample_kernel(pt_ref, *refs, pages, nq):
    k_refs, v_refs = refs[:pages], refs[pages:2 * pages]
    qt_ref, new_ref, o_ref, kbuf, vbuf, acc_scr, later_scr = refs[2 * pages:]
    jb = pl.program_id(1)
    page = new_ref.shape[1]
    rows = H_SB * nq
    tri = _later_keys_tri(LANES)
    qt = qt_ref[0]

    def sweep(k, v, valid):
        z = _dot_nt(qt, k) * SCALE
        a, later = _sb_weights(z, valid, later_scr[...], tri)
        acc_scr[...] += _dot(a.astype(BF16), v)
        later_scr[...] = later

    @pl.when(jb == 0)
    def _():
        acc_scr[...] = jnp.zeros_like(acc_scr)
        later_scr[...] = jnp.zeros_like(later_scr)
        key = lax.broadcasted_iota(jnp.int32, (rows, page), 1)
        qry = lax.broadcasted_iota(jnp.int32, (rows, page), 0) % nq
        sweep(new_ref[0, :, :SB_W].astype(BF16), new_ref[0, :, SB_W:].astype(BF16), key < qry)

    for src_refs, buf in ((k_refs, kbuf), (v_refs, vbuf)):
        for i, pr in enumerate(src_refs):
            by_head = jnp.swapaxes(pr[0, :, 0], 0, 1).astype(BF16)
            for h in range(H_SB):
                buf[i * page:(i + 1) * page, h * HEAD_DIM:(h + 1) * HEAD_DIM] = by_head[h]
    sweep(kbuf[...], vbuf[...], None)

    @pl.when(jb == pl.num_programs(1) - 1)
    def _():
        acc = acc_scr[...]
        for h in range(H_SB):
            o_ref[0, :, h * HEAD_DIM:(h + 1) * HEAD_DIM] = acc[h * nq:(h + 1) * nq,
                                                               h * HEAD_DIM:(h + 1) * HEAD_DIM]


def sb_sample(cache_sb, page_table, q_sb, kv_new):
    n_pool, page = cache_sb.shape[:2]
    bsz, n_pages = page_table.shape
    nq = q_sb.shape[1]
    pages = min(PAGES_PER_STEP_SB, n_pages)
    nsteps = n_pages // pages
    half = (1, page, 1, H_SB, HEAD_DIM)
    qh = q_sb.reshape(bsz, nq, H_SB, HEAD_DIM).transpose(0, 2, 1, 3)
    qt = (qh[:, :, :, None, :] * jnp.eye(H_SB, dtype=BF16)[None, :, None, :, None])
    qt = qt.reshape(bsz, H_SB * nq, SB_W)
    new_page = jnp.pad(kv_new, ((0, 0), (0, page - nq), (0, 0)))

    def page_map(i, c):
        return lambda b, jb, pt: (pt[b, n_pages - (jb + 1) * pages + i], 0, c, 0, 0)

    grid_spec = pltpu.PrefetchScalarGridSpec(
        num_scalar_prefetch=1,
        grid=(bsz, nsteps),
        in_specs=[pl.BlockSpec(half, page_map(i, c)) for c in range(2) for i in range(pages)]
        + [pl.BlockSpec((1, H_SB * nq, SB_W), lambda b, jb, pt: (b, 0, 0)),
           pl.BlockSpec((1, page, 2 * SB_W), lambda b, jb, pt: (b, 0, 0))],
        out_specs=pl.BlockSpec((1, nq, SB_W), lambda b, jb, pt: (b, 0, 0)),
        scratch_shapes=[pltpu.VMEM((pages * page, SB_W), BF16), pltpu.VMEM((pages * page, SB_W), BF16),
                        pltpu.VMEM((H_SB * nq, SB_W), F32), pltpu.VMEM((H_SB * nq, 1), F32)],
    )
    return pl.pallas_call(
        functools.partial(_sb_sample_kernel, pages=pages, nq=nq),
        out_shape=jax.ShapeDtypeStruct((bsz, nq, SB_W), F32),
        grid_spec=grid_spec,
        compiler_params=_cparams(("arbitrary", "arbitrary")),
        name="sb_sample",
    )(page_table, *([cache_sb] * (2 * pages)), qt, new_page)


def _nsa_sample_sel_kernel(q_ref, kc_ref, vc_ref, st_ref, new_ref, ocmp_ref, owin_ref, selx_ref,
                           *, nq, nbp, past, topk):
    rows = HPG * nq
    slots = 2 * G_NSA
    wb = st_ref.shape[1] // slots
    t_in = lax.broadcasted_iota(jnp.int32, (nq, 1), 0)
    qpos4 = jnp.concatenate([past + t_in] * HPG, axis=0)
    kb = lax.broadcasted_iota(jnp.int32, (nbp, nbp * CMP_BLK), 0)
    kl = lax.broadcasted_iota(jnp.int32, (nbp, nbp * CMP_BLK), 1)
    expand = jnp.where(kl // CMP_BLK == kb, 1.0, 0.0).astype(BF16)
    lane_w = lax.broadcasted_iota(jnp.int32, (rows, wb), 1)
    lane_n = lax.broadcasted_iota(jnp.int32, (rows, new_ref.shape[1]), 1)
    for g in range(G_NSA):
        q4 = jnp.concatenate([q_ref[0, :, (g * HPG + n) * HEAD_DIM:(g * HPG + n + 1) * HEAD_DIM]
                              for n in range(HPG)], axis=0)
        o_cmp, rank = _cmp_branch(q4, qpos4, kc_ref[0, 0, g], vc_ref[0, 0, g], nq, nbp)
        sel = jnp.where(rank < topk, 1.0, 0.0).astype(BF16)
        selx_ref[0, g] = _dot(sel, expand)
        kcol, vcol = g * HEAD_DIM, (G_NSA + g) * HEAD_DIM
        state = _flash_init(rows)
        d = qpos4 - (past - wb + lane_w)
        state = _flash_tile(q4, st_ref[0, pl.ds(g, wb, stride=slots), :].astype(BF16),
                            st_ref[0, pl.ds(G_NSA + g, wb, stride=slots), :].astype(BF16),
                            (d >= 0) & (d < WINDOW), state)
        d = qpos4 - (past + lane_n)
        state = _flash_tile(q4, new_ref[0, :, kcol:kcol + HEAD_DIM].astype(BF16),
                            new_ref[0, :, vcol:vcol + HEAD_DIM].astype(BF16),
                            (d >= 0) & (d < WINDOW) & (lane_n < nq), state)
        o_win = _flash_out(state)
        for n in range(HPG):
            h = g * HPG + n
            ocmp_ref[0, :, h * HEAD_DIM:(h + 1) * HEAD_DIM] = o_cmp[n * nq:(n + 1) * nq]
            owin_ref[0, :, h * HEAD_DIM:(h + 1) * HEAD_DIM] = o_win[n * nq:(n + 1) * nq]


def nsa_sample_select(q_nsa, kvc, state_win, win_new, past):
    bsz, nq = q_nsa.shape[:2]
    nbp = kvc.shape[3]
    nb = -(-(past + nq) // CMP_BLK)
    wrows = state_win.shape[1]
    return pl.pallas_call(
        functools.partial(_nsa_sample_sel_kernel, nq=nq, nbp=nbp, past=past, topk=min(TOP_BLOCKS, nb)),
        out_shape=[jax.ShapeDtypeStruct((bsz, nq, NSA_W), F32),
                   jax.ShapeDtypeStruct((bsz, nq, NSA_W), F32),
                   jax.ShapeDtypeStruct((bsz, G_NSA, nq, nbp * CMP_BLK), F32)],
        grid=(bsz,),
        in_specs=[pl.BlockSpec((1, nq, NSA_W), lambda b: (b, 0, 0)),
                  pl.BlockSpec((1, 1, G_NSA, nbp, HEAD_DIM), lambda b: (b, 0, 0, 0, 0)),
                  pl.BlockSpec((1, 1, G_NSA, nbp, HEAD_DIM), lambda b: (b, 1, 0, 0, 0)),
                  pl.BlockSpec((1, wrows, HEAD_DIM), lambda b: (b, 0, 0)),
                  pl.BlockSpec((1,) + win_new.shape[1:], lambda b: (b, 0, 0))],
        out_specs=[pl.BlockSpec((1, nq, NSA_W), lambda b: (b, 0, 0)),
                   pl.BlockSpec((1, nq, NSA_W), lambda b: (b, 0, 0)),
                   pl.BlockSpec((1, G_NSA, nq, nbp * CMP_BLK), lambda b: (b, 0, 0, 0))],
        compiler_params=_cparams(("arbitrary",)),
        name="nsa_sample_select",
    )(q_nsa, kvc, kvc, state_win, win_new)


def _nsa_sample_slc_kernel(pt_ref, *refs, pages, nq, past):
    page_refs = refs[:pages]
    (q_ref, new_ref, selx_ref, selx_new_ref, ocmp_ref, owin_ref, gate_ref, o_ref,
     kbuf, vbuf, m_scr, l_scr, acc_scr) = refs[pages:]
    jb = pl.program_id(1)
    rows = HPG * nq
    slots = 2 * G_NSA
    page = page_refs[0].shape[1] // slots
    t_in = lax.broadcasted_iota(jnp.int32, (nq, 1), 0)
    qpos4 = jnp.concatenate([past + t_in] * HPG, axis=0)
    q4s = [jnp.concatenate([q_ref[0, :, (g * HPG + n) * HEAD_DIM:(g * HPG + n + 1) * HEAD_DIM]
                            for n in range(HPG)], axis=0) for g in range(G_NSA)]

    @pl.when(jb == 0)
    def _():
        m_scr[...] = jnp.full(m_scr.shape, NEG_INF, F32)
        l_scr[...] = jnp.zeros_like(l_scr)
        acc_scr[...] = jnp.zeros_like(acc_scr)

    def sweep(g, k, v, sx, causal):
        mask = jnp.concatenate([sx] * HPG, axis=0) > 0.5
        if causal is not None:
            mask = mask & causal
        state = (m_scr[g], l_scr[g], acc_scr[g])
        m_scr[g], l_scr[g], acc_scr[g] = _flash_tile(q4s[g], k, v, mask, state)

    for g in range(G_NSA):
        for i, pr in enumerate(page_refs):
            dst = slice(i * page, (i + 1) * page)
            kbuf[g, dst, :] = pr[0, pl.ds(g, page, stride=slots), :].astype(BF16)
            vbuf[g, dst, :] = pr[0, pl.ds(G_NSA + g, page, stride=slots), :].astype(BF16)
        sweep(g, kbuf[g], vbuf[g], selx_ref[0, g], None)

    @pl.when(jb == pl.num_programs(1) - 1)
    def _():
        nrow = new_ref.shape[1]
        lane_n = lax.broadcasted_iota(jnp.int32, (rows, nrow), 1)
        for g in range(G_NSA):
            kcol, vcol = g * HEAD_DIM, (G_NSA + g) * HEAD_DIM
            sweep(g, new_ref[0, :, kcol:kcol + HEAD_DIM].astype(BF16),
                  new_ref[0, :, vcol:vcol + HEAD_DIM].astype(BF16),
                  selx_new_ref[0, g, :, 0:nrow], ((past + lane_n) <= qpos4) & (lane_n < nq))
        gates = gate_ref[0]
        for g in range(G_NSA):
            o_slc = _flash_out((m_scr[g], l_scr[g], acc_scr[g]))
            for n in range(HPG):
                h = g * HPG + n
                cols = slice(h * HEAD_DIM, (h + 1) * HEAD_DIM)
                o_ref[0, :, cols] = (gates[:, 3 * h:3 * h + 1] * ocmp_ref[0, :, cols]
                                     + gates[:, 3 * h + 1:3 * h + 2] * o_slc[n * nq:(n + 1) * nq]
                                     + gates[:, 3 * h + 2:3 * h + 3] * owin_ref[0, :, cols])


def nsa_sample_slc(cache_slc, page_table, q_nsa, slc_new, selx, o_cmp, o_win, gates):
    n_pool, page = cache_slc.shape[:2]
    bsz, n_pages = page_table.shape
    nq = q_nsa.shape[1]
    past = n_pages * page
    pages = min(PAGES_PER_STEP_SLC, n_pages)
    prow = page * 2 * G_NSA
    cache = cache_slc.reshape(n_pool, prow, HEAD_DIM)
    rows = HPG * nq

    def page_map(i):
        return lambda b, jb, pt: (pt[b, jb * pages + i], 0, 0)

    per_b = lambda b, jb, pt: (b, 0, 0)
    grid_spec = pltpu.PrefetchScalarGridSpec(
        num_scalar_prefetch=1,
        grid=(bsz, n_pages // pages),
        in_specs=[pl.BlockSpec((1, prow, HEAD_DIM), page_map(i)) for i in range(pages)]
        + [pl.BlockSpec((1, nq, NSA_W), per_b),
           pl.BlockSpec((1,) + slc_new.shape[1:], per_b),
           pl.BlockSpec((1, G_NSA, nq, pages * page), lambda b, jb, pt: (b, 0, 0, jb)),
           pl.BlockSpec((1, G_NSA, nq, LANES), lambda b, jb, pt: (b, 0, 0, past // LANES)),
           pl.BlockSpec((1, nq, NSA_W), per_b),
           pl.BlockSpec((1, nq, NSA_W), per_b),
           pl.BlockSpec((1, nq, GATE_W), per_b)],
        out_specs=pl.BlockSpec((1, nq, NSA_W), per_b),
        scratch_shapes=[pltpu.VMEM((G_NSA, pages * page, HEAD_DIM), BF16),
                        pltpu.VMEM((G_NSA, pages * page, HEAD_DIM), BF16),
                        pltpu.VMEM((G_NSA, rows, 1), F32), pltpu.VMEM((G_NSA, rows, 1), F32),
                        pltpu.VMEM((G_NSA, rows, HEAD_DIM), F32)],
    )
    return pl.pallas_call(
        functools.partial(_nsa_sample_slc_kernel, pages=pages, nq=nq, past=past),
        out_shape=jax.ShapeDtypeStruct((bsz, nq, NSA_W), F32),
        grid_spec=grid_spec,
        compiler_params=_cparams(("arbitrary", "arbitrary")),
        name="nsa_sample_slc",
    )(page_table, *([cache] * pages), q_nsa, slc_new, selx, selx, o_cmp, o_win, gates)


def _outproj_kernel(osb_ref, onsa_ref, gsb_ref, gnsa_ref, wa_ref, wb_ref, x_ref, gate_ref,
                    g2_ref, shift_ref, scale_ref, x1_ref, h2_ref):
    a = _rms(osb_ref[...], gsb_ref[...]).astype(BF16)
    b = _rms(onsa_ref[...], gnsa_ref[...]).astype(BF16)
    mixed = _dot(a, wa_ref[...]) + _dot(b, wb_ref[...])
    x1 = x_ref[...] + gate_ref[0] * mixed
    x1_ref[...] = x1
    h2_ref[...] = (_rms(x1, g2_ref[...]) * (1.0 + scale_ref[0]) + shift_ref[0]).astype(BF16)


def out_projection(o_sb, o_nsa, g_sb, g_nsa, w_out_bf, x2d, gate, g2, shift, scale, tm, mod_map):
    r, d = x2d.shape
    mod_block = (1,) + gate.shape[1:]
    mod_spec = pl.BlockSpec(mod_block, lambda i: (mod_map(i), 0, 0))
    row = lambda w: pl.BlockSpec((tm, w), lambda i: (i, 0))
    const = lambda shp: pl.BlockSpec(shp, lambda i: (0,) * len(shp))
    return pl.pallas_call(
        _outproj_kernel,
        out_shape=[jax.ShapeDtypeStruct((r, d), F32), jax.ShapeDtypeStruct((r, d), BF16)],
        grid=(r // tm,),
        in_specs=[row(SB_W), row(NSA_W), const((1, SB_W)), const((1, NSA_W)),
                  pl.BlockSpec((SB_W, d), lambda i: (0, 0)),
                  pl.BlockSpec((NSA_W, d), lambda i: (1, 0)),
                  row(d), mod_spec, const((1, d)), mod_spec, mod_spec],
        out_specs=[row(d), row(d)],
        compiler_params=_cparams(("arbitrary",)),
        name="out_projection",
    )(o_sb, o_nsa, g_sb, g_nsa, w_out_bf, w_out_bf, x2d, gate, g2, shift, scale)


def _peer_score_kernel(h_ref, wq_ref, k1_ref, k2_ref, s_ref):
    q = _dot(h_ref[...], wq_ref[...]).astype(BF16)
    half = D_QUERY // 2
    nchunk = h_ref.shape[0] // LANES
    for h in range(PEER_HEADS):
        for c, k_ref in enumerate((k1_ref, k2_ref)):
            qh = q[:, h * D_QUERY + c * half:h * D_QUERY + (c + 1) * half]
            st = _dot_nt(k_ref[h], qh)
            for ch in range(nchunk):
                s_ref[c, h, ch] = st[:, ch * LANES:(ch + 1) * LANES]


def peer_scores(h2, wq_bf, k1_bf, k2_bf):
    r, d = h2.shape
    tm = PEER_TN
    nchunk = tm // LANES
    return pl.pallas_call(
        _peer_score_kernel,
        out_shape=jax.ShapeDtypeStruct((2, PEER_HEADS, r // LANES, N_KEYS, LANES), F32),
        grid=(r // tm,),
        in_specs=[pl.BlockSpec((tm, d), lambda i: (i, 0)),
                  pl.BlockSpec(wq_bf.shape, lambda i: (0, 0)),
                  pl.BlockSpec(k1_bf.shape, lambda i: (0, 0, 0)),
                  pl.BlockSpec(k2_bf.shape, lambda i: (0, 0, 0))],
        out_specs=pl.BlockSpec((2, PEER_HEADS, nchunk, N_KEYS, LANES), lambda i: (0, 0, i, 0, 0)),
        compiler_params=_cparams(("arbitrary",)),
        name="peer_scores",
    )(h2, wq_bf, k1_bf, k2_bf)


def _top_values(x, n, scr, with_rank):
    rank = jnp.full(x.shape, float(n), F32) if with_rank else None
    for it in range(n):
        m = jnp.max(x, axis=0, keepdims=True)
        scr[it:it + 1, :] = m
        hit = x == m
        if with_rank:
            rank = jnp.where(hit, float(it), rank)
        x = jnp.where(hit, NEG_INF, x)
    return scr[0:n, :], rank


def _pack_bf16(x):
    return pltpu.bitcast(x.astype(BF16), jnp.int32)


def _peer_select_kernel(s_ref, r2_ref, e2_ref, cnt_ref, e1_ref, v1_scr, v2_scr):
    k = PEER_TOPK
    nchunk = s_ref.shape[2]

    def body(idx, _):
        h, ch = idx // nchunk, idx % nchunk
        s1 = s_ref[0, h, ch]
        s2 = s_ref[1, h, ch]
        v1, _ = _top_values(s1, k, v1_scr, False)
        v2, r2 = _top_values(s2, k, v2_scr, True)
        x = jnp.concatenate([v1[0:1] + v2] + [v1[i:i + 1] + v2[0:k // 2] for i in range(1, k // 2)]
                            + [v1[k // 2:k] + v2[0:1]], axis=0)
        xs = x
        for _ in range(k - 1):
            xs = jnp.where(xs == jnp.max(xs, axis=0, keepdims=True), NEG_INF, xs)
        tau = jnp.max(xs, axis=0, keepdims=True)
        vmax = v1[0:1] + v2[0:1]
        chosen = x >= tau
        z = jnp.sum(jnp.where(chosen, jnp.exp(x - vmax), 0.0), axis=0, keepdims=True)
        c = jnp.where(chosen, 1.0, 0.0)
        cnt_rows = ([jnp.sum(c[0:k], axis=0, keepdims=True)]
                    + [jnp.sum(c[k + (k // 2) * (i - 1):k + (k // 2) * i], axis=0, keepdims=True)
                       for i in range(1, k // 2)])
        cnt_tab = jnp.concatenate(cnt_rows + [c[k + (k // 2) * (k // 2 - 1):]], axis=0)
        cnt = jnp.zeros_like(s1)
        for i in range(k):
            cnt = jnp.where(s1 == v1[i:i + 1], cnt_tab[i:i + 1], cnt)
        r2_ref[h, ch] = _pack_bf16(r2)
        e2_ref[h, ch] = _pack_bf16(jnp.exp(s2 - v2[0:1]))
        cnt_ref[h, ch] = cnt
        e1_ref[h, ch] = jnp.exp(s1 - v1[0:1]) / z
        return 0

    lax.fori_loop(0, PEER_HEADS * nchunk, body, 0)


def peer_select(s):
    _, heads, nch_all, keys, lanes = s.shape
    nchunk = PEER_TN // LANES
    blk = (heads, nchunk, keys, lanes)
    pblk = (heads, nchunk, keys // 2, lanes)
    full = lambda shp: (shp[0], nch_all) + shp[2:]
    return pl.pallas_call(
        _peer_select_kernel,
        out_shape=[jax.ShapeDtypeStruct(full(pblk), jnp.int32),
                   jax.ShapeDtypeStruct(full(pblk), jnp.int32),
                   jax.ShapeDtypeStruct(full(blk), F32),
                   jax.ShapeDtypeStruct(full(blk), F32)],
        grid=(nch_all // nchunk,),
        in_specs=[pl.BlockSpec((2,) + blk, lambda i: (0, 0, i, 0, 0))],
        out_specs=[pl.BlockSpec(pblk, lambda i: (0, i, 0, 0)),
                   pl.BlockSpec(pblk, lambda i: (0, i, 0, 0)),
                   pl.BlockSpec(blk, lambda i: (0, i, 0, 0)),
                   pl.BlockSpec(blk, lambda i: (0, i, 0, 0))],
        scratch_shapes=[pltpu.VMEM((PEER_TOPK, LANES), F32), pltpu.VMEM((PEER_TOPK, LANES), F32)],
        compiler_params=_cparams(("arbitrary",)),
        name="peer_select",
    )(s)


def _row_bf16(row):
    packed_rows = 2 * SUBLANES
    return pltpu.repeat(jnp.broadcast_to(row, (packed_rows, LANES)).astype(BF16), N_KEYS // packed_rows, axis=0)


def _peer_expert_kernel(h_ref, u_ref, vt_ref, r2_ref, e2_ref, cnt_ref, e1_ref, o_ref, ga0, ga1, g0, g1):
    e = pl.program_id(1)
    ne = pl.num_programs(1) - 2
    nchunk = h_ref.shape[0] // LANES

    @pl.when(e == 0)
    def _():
        o_ref[...] = jnp.zeros_like(o_ref)
        ga0[...] = jnp.zeros_like(ga0)
        g1[...] = jnp.zeros_like(g1)

    def step(g_cur, g_prev, ga_prev, ga_prev2):
        et = jnp.minimum(e, ne - 1)
        zero = jnp.zeros((N_KEYS, LANES), BF16)
        for ch in range(nchunk):
            for al in range(PEER_AT):
                a = et * PEER_AT + al
                gsum = zero
                for h in range(PEER_HEADS):
                    hit = pltpu.bitcast(r2_ref[h, ch], BF16) < _row_bf16(cnt_ref[h, ch, pl.ds(a, 1), :])
                    e2 = pltpu.bitcast(e2_ref[h, ch], BF16)
                    gsum = gsum + jnp.where(hit, e2, zero) * _row_bf16(e1_ref[h, ch, pl.ds(a, 1), :])
                g_cur[al * N_KEYS:(al + 1) * N_KEYS, ch * LANES:(ch + 1) * LANES] = gsum
        o_ref[...] += _dot(vt_ref[...], ga_prev2[...])
        act = _gelu(_dot_nt(u_ref[...], h_ref[...]))
        ga_prev[...] = act.astype(BF16) * g_prev[...]

    @pl.when(e % 2 == 0)
    def _():
        step(g0, g1, ga1, ga0)

    @pl.when(e % 2 == 1)
    def _():
        step(g1, g0, ga0, ga1)


def peer_experts(h2, u_bf, vt_bf, r2p, e2p, cnt, e1):
    r, d = h2.shape
    n_exp = u_bf.shape[0]
    tn = PEER_TN
    nchunk = tn // LANES
    et = PEER_AT * N_KEYS
    sblk = (PEER_HEADS, nchunk, N_KEYS, LANES)
    pblk = (PEER_HEADS, nchunk, N_KEYS // 2, LANES)
    ne = n_exp // et
    return pl.pallas_call(
        _peer_expert_kernel,
        out_shape=jax.ShapeDtypeStruct((d, r), F32),
        grid=(r // tn, ne + 2),
        in_specs=[pl.BlockSpec((tn, d), lambda t, e: (t, 0)),
                  pl.BlockSpec((et, d), lambda t, e: (jnp.clip(e - 1, 0, ne - 1), 0)),
                  pl.BlockSpec((d, et), lambda t, e: (0, jnp.clip(e - 2, 0, ne - 1))),
                  pl.BlockSpec(pblk, lambda t, e: (0, t, 0, 0)),
                  pl.BlockSpec(pblk, lambda t, e: (0, t, 0, 0)),
                  pl.BlockSpec(sblk, lambda t, e: (0, t, 0, 0)),
                  pl.BlockSpec(sblk, lambda t, e: (0, t, 0, 0))],
        out_specs=pl.BlockSpec((d, tn), lambda t, e: (0, t)),
        scratch_shapes=[pltpu.VMEM((et, tn), BF16), pltpu.VMEM((et, tn), BF16),
                        pltpu.VMEM((et, tn), BF16), pltpu.VMEM((et, tn), BF16)],
        compiler_params=_cparams(("arbitrary", "arbitrary")),
        name="peer_experts",
    )(h2, u_bf, vt_bf, r2p, e2p, cnt, e1)


def _final_kernel(x1_ref, pt_ref, gate_ref, g_ref, y_ref):
    x2 = x1_ref[...] + gate_ref[0] * pt_ref[...].T
    y_ref[...] = _rms(x2, g_ref[...])


def final_norm(x1, peer_t, col0, gate, g, tm, mod_map):
    r, d = x1.shape
    mod_block = (1,) + gate.shape[1:]
    return pl.pallas_call(
        _final_kernel,
        out_shape=jax.ShapeDtypeStruct((r, d), F32),
        grid=(r // tm,),
        in_specs=[pl.BlockSpec((tm, d), lambda i: (i, 0)),
                  pl.BlockSpec((d, tm), lambda i: (0, col0 // tm + i)),
                  pl.BlockSpec(mod_block, lambda i: (mod_map(i), 0, 0)),
                  pl.BlockSpec((1, d), lambda i: (0, 0))],
        out_specs=pl.BlockSpec((tm, d), lambda i: (i, 0)),
        compiler_params=_cparams(("arbitrary",)),
        name="final_norm",
    )(x1, peer_t, gate, g)


def _layer(x_prompt, x_sample, cache_sb, cache_cmp, cache_slc, state_win, page_table, c_prompt,
           c_sample, w_ada, b_ada, norm1_g, w_in, cmp_pe_k, cmp_w1_k, cmp_w2_k, cmp_pe_v, cmp_w1_v,
           cmp_w2_v, out_g_sb, out_g_nsa, w_out, norm2_g, peer_wq, peer_k1, peer_k2, peer_u, peer_v):
    b, t, d = x_prompt.shape
    bs, nq, _ = x_sample.shape
    n_pages, page = page_table.shape[1], cache_sb.shape[1]
    past = n_pages * page
    rp, rs = b * t, bs * nq

    w_in_bf = jnp.concatenate([w_in.astype(BF16), jnp.zeros((d, IN_NT * IN_TN - D_IN), BF16)], axis=1)
    w_out_bf = w_out.astype(BF16)
    wq_bf = peer_wq.astype(BF16)
    k1_bf, k2_bf = peer_k1.astype(BF16), peer_k2.astype(BF16)
    u_bf = peer_u.astype(BF16)
    vt_bf = peer_v.astype(BF16).T
    pe = jnp.stack([cmp_pe_k, cmp_pe_v])
    w1 = jnp.stack([cmp_w1_k, cmp_w1_v]).astype(BF16).reshape(2, CMP_BLK * HEAD_DIM, HEAD_DIM)
    w2 = jnp.stack([cmp_w2_k, cmp_w2_v]).astype(BF16)
    row = lambda v: v.reshape(1, -1)

    n_c = b + bs
    c_all = jnp.pad(jnp.concatenate([c_prompt, c_sample]), ((0, -n_c % 16), (0, 0))).astype(BF16)
    mod = adaln_mod(c_all, w_ada, row(b_ada))
    mod_p = [mod[:b, k * d:(k + 1) * d].reshape(b, 1, d) for k in range(6)]
    mod_s = [jnp.repeat(mod[b:n_c, k * d:(k + 1) * d], nq, axis=0).reshape(1, rs, d) for k in range(6)]

    tm_p = 512 if t % 512 == 0 else t
    xp = x_prompt.reshape(rp, d)
    tabs_p = _rope_tables(jnp.arange(t))
    qsb_p, kvsb_p, qnsa_p, kvcmp_p, kvslc_p, kvwin_p, gates_p = in_projection(
        xp, row(norm1_g), mod_p[0], mod_p[1], w_in_bf, tabs_p, tm_p,
        lambda i: i // (t // tm_p), lambda i: i % (t // tm_p))
    osb_p = sb_prompt(qsb_p, kvsb_p, b, t)
    kvc_p = compress_prompt(kvcmp_p, pe, w1, w2, b, t)
    onsa_p = nsa_prompt(qnsa_p, kvc_p, kvslc_p, kvwin_p, gates_p, b, t)
    tm_o = 512 if t % 512 == 0 else t
    x1_p, h2_p = out_projection(osb_p, onsa_p, row(out_g_sb), row(out_g_nsa), w_out_bf, xp,
                                mod_p[2], row(norm2_g), mod_p[3], mod_p[4], tm_o,
                                lambda i: i // (t // tm_o))

    xs = x_sample.reshape(rs, d)
    tabs_s = _rope_tables(past + jnp.tile(jnp.arange(nq), bs))
    qsb_s, kvsb_s, qnsa_s, kvcmp_s, kvslc_s, kvwin_s, gates_s = in_projection(
        xs, row(norm1_g), mod_s[0], mod_s[1], w_in_bf, tabs_s, rs, lambda i: 0, lambda i: 0)
    osb_s = sb_sample(cache_sb, page_table, qsb_s.reshape(bs, nq, SB_W), kvsb_s.reshape(bs, nq, 2 * SB_W))
    kvcmp_s3 = kvcmp_s.reshape(bs, nq, KV_W)
    kvc_s = compress_sample(cache_cmp, page_table, kvcmp_s3, pe, w1, w2)
    pad8 = lambda a: jnp.pad(a.reshape(bs, nq, KV_W), ((0, 0), (0, -nq % SUBLANES), (0, 0)))
    qnsa_s3 = qnsa_s.reshape(bs, nq, NSA_W)
    st_win = state_win.reshape(bs, state_win.shape[1] * 2 * G_NSA, HEAD_DIM)
    ocmp_s, owin_s, selx = nsa_sample_select(qnsa_s3, kvc_s, st_win, pad8(kvwin_s), past)
    onsa_s = nsa_sample_slc(cache_slc, page_table, qnsa_s3, pad8(kvslc_s), selx, ocmp_s, owin_s,
                            gates_s.reshape(bs, nq, GATE_W))
    x1_s, h2_s = out_projection(osb_s.reshape(rs, SB_W), onsa_s.reshape(rs, NSA_W), row(out_g_sb),
                                row(out_g_nsa), w_out_bf, xs, mod_s[2], row(norm2_g), mod_s[3],
                                mod_s[4], rs, lambda i: 0)

    r = rp + rs
    r_pad = -(-r // PEER_TN) * PEER_TN
    h2 = jnp.pad(jnp.concatenate([h2_p, h2_s]), ((0, r_pad - r), (0, 0)))
    s = peer_scores(h2, wq_bf, k1_bf, k2_bf)
    r2p, e2p, cnt, e1 = peer_select(s)
    peer_t = peer_experts(h2, u_bf, vt_bf, r2p, e2p, cnt, e1)

    states_p = (kvsb_p.reshape(b, t, 2, H_SB, HEAD_DIM), kvcmp_p.reshape(b, t, 2, G_NSA, HEAD_DIM),
                kvslc_p.reshape(b, t, 2, G_NSA, HEAD_DIM),
                kvwin_p.reshape(b, t, 2, G_NSA, HEAD_DIM)[:, t - min(WINDOW, t):])
    kvwin_s5 = kvwin_s.reshape(bs, nq, 2, G_NSA, HEAD_DIM)
    states_s = (kvsb_s.reshape(bs, nq, 2, H_SB, HEAD_DIM), kvcmp_s.reshape(bs, nq, 2, G_NSA, HEAD_DIM),
                kvslc_s.reshape(bs, nq, 2, G_NSA, HEAD_DIM),
                jnp.concatenate([state_win, kvwin_s5], axis=1)[:, nq:])
    return (x1_p, mod_p[5], x1_s, mod_s[5], peer_t, rp), states_p, states_s


def kernel(x_prompt, x_sample, cache_sb, cache_cmp, cache_slc, state_win, page_table, c_prompt, c_sample, w_ada, b_ada, norm1_g, w_in, cmp_pe_k, cmp_w1_k, cmp_w2_k, cmp_pe_v, cmp_w1_v, cmp_w2_v, out_g_sb, out_g_nsa, w_out, norm2_g, peer_wq, peer_k1, peer_k2, peer_u, peer_v, final_g):
    depth = w_ada.shape[0]
    assert depth == 1, "single-layer trunk"
    b, t, d = x_prompt.shape
    bs, nq, _ = x_sample.shape
    (x1_p, gate_p, x1_s, gate_s, peer_t, rp), st_p, st_s = _layer(
        x_prompt, x_sample, cache_sb[0], cache_cmp[0], cache_slc[0], state_win[0], page_table,
        c_prompt, c_sample, w_ada[0], b_ada[0], norm1_g[0], w_in[0], cmp_pe_k[0], cmp_w1_k[0],
        cmp_w2_k[0], cmp_pe_v[0], cmp_w1_v[0], cmp_w2_v[0], out_g_sb[0], out_g_nsa[0], w_out[0],
        norm2_g[0], peer_wq[0], peer_k1[0], peer_k2[0], peer_u[0], peer_v[0])
    fg = final_g.reshape(1, d)
    tm_f = 256
    y_p = final_norm(x1_p, peer_t, 0, gate_p, fg, tm_f, lambda i: i // (t // tm_f)).reshape(b, t, d)
    rs = bs * nq
    y_s = final_norm(x1_s, peer_t, rp, gate_s, fg, rs, lambda i: 0).reshape(bs, nq, d)
    return (y_p, y_s, st_p[0][None], st_s[0][None], st_p[1][None], st_s[1][None],
            st_p[2][None], st_s[2][None], st_p[3][None], st_s[3][None])
```

```python
import functools

import jax
import jax.numpy as jnp
from jax import lax
from jax.experimental import pallas as pl
from jax.experimental.pallas import tpu as pltpu

F32 = jnp.float32
BF16 = jnp.bfloat16

HEAD_DIM = 128
H_SB = 8
H_NSA = 8
G_NSA = 2
HPG = H_NSA // G_NSA
SB_W = H_SB * HEAD_DIM
NSA_W = H_NSA * HEAD_DIM
KV_W = 2 * G_NSA * HEAD_DIM
D_IN = 3 * SB_W + NSA_W + 3 * KV_W + 3 * H_NSA
SCALE = HEAD_DIM ** -0.5
ROT_HALF = HEAD_DIM // 8
ROPE_THETA = 500000.0
CMP_BLK = 64
TOP_BLOCKS = 16
WINDOW = 512
FORCE_SCORE = 1e9
N_KEYS = 128
PEER_HEADS = 8
PEER_TOPK = 16
D_QUERY = 256
EPS = 1e-6
NEG_INF = -1e30

LANES = 128
SUBLANES = 8
VMEM_LIMIT = 56 * 1024 * 1024

MXU_N = 256
IN_TN = 2 * MXU_N
IN_NT = -(-D_IN // IN_TN)
GATE_W = LANES
PEER_TN = 3 * MXU_N
CMP_PITCH = CMP_BLK * 2 * G_NSA + SUBLANES
PEER_AT = 4
PAGES_PER_STEP_SB = 16
PAGES_PER_STEP_NSA = 16
PAGES_PER_STEP_SLC = 32


def _cparams(sem):
    return pltpu.CompilerParams(dimension_semantics=sem, vmem_limit_bytes=VMEM_LIMIT)


def _dot(a, b):
    return jnp.dot(a, b, preferred_element_type=F32)


def _dot_nt(a, b):
    return lax.dot_general(a, b, (((1,), (1,)), ((), ())), preferred_element_type=F32)


def _dot_tn(a, b):
    return lax.dot_general(a, b, (((0,), (0,)), ((), ())), preferred_element_type=F32)


def _gelu(x):
    return 0.5 * x * (1.0 + jnp.tanh(0.7978845608028654 * (x + 0.044715 * (x * x * x))))


def _rms(x, g):
    return x * lax.rsqrt(jnp.mean(x * x, axis=-1, keepdims=True) + EPS) * g


def _softplus(z):
    return jnp.maximum(z, 0.0) + jnp.log(1.0 + jnp.exp(-jnp.abs(z)))


def _split_bf16(x):
    hi = x.astype(BF16)
    lo = (x - hi.astype(F32)).astype(BF16)
    return hi, lo


def _mod_kernel(c_ref, w_ref, b_ref, o_ref):
    o_ref[...] = _dot(c_ref[...], w_ref[...].astype(BF16)) + b_ref[...]


def adaln_mod(c_bf, w_ada, b_ada):
    m, d = c_bf.shape
    n = w_ada.shape[1]
    tn = 1024
    return pl.pallas_call(
        _mod_kernel,
        out_shape=jax.ShapeDtypeStruct((m, n), F32),
        grid=(n // tn,),
        in_specs=[pl.BlockSpec((m, d), lambda j: (0, 0)),
                  pl.BlockSpec((d, tn), lambda j: (0, j)),
                  pl.BlockSpec((1, tn), lambda j: (0, j))],
        out_specs=pl.BlockSpec((m, tn), lambda j: (0, j)),
        compiler_params=_cparams(("arbitrary",)),
        name="adaln_mod",
    )(c_bf, w_ada, b_ada)


_IN_SEGMENTS = (
    (0, 2, 0, "plain"),
    (2, 6, 1, "plain"),
    (6, 8, 2, "rope"),
    (8, 9, 3, "rope_k"),
    (9, 10, 4, "rope_k"),
    (10, 11, 5, "rope_k"),
    (11, 12, 6, "gate"),
)
_IN_OUT_RANGES = ((0, 2), (2, 6), (6, 8), (8, 9), (9, 10), (10, 11), (11, 12))


def _rope(p, rc, ra, rb):
    w = p.shape[1]
    return p * rc + pltpu.roll(p, ROT_HALF, 1) * ra + pltpu.roll(p, w - ROT_HALF, 1) * rb


def _inproj_kernel(x_ref, g_ref, shift_ref, scale_ref, w_ref, rc_ref, ra_ref, rb_ref,
                   *rest):
    outs, h_scr = rest[:7], rest[7]
    j = pl.program_id(1)

    @pl.when(j == 0)
    def _():
        h = _rms(x_ref[...], g_ref[...]) * (1.0 + scale_ref[0]) + shift_ref[0]
        h_scr[...] = h.astype(BF16)

    p = _dot(h_scr[...], w_ref[...])
    kw = G_NSA * HEAD_DIM
    for lo, hi, slot, kind in _IN_SEGMENTS:
        @pl.when((j >= lo) & (j < hi))
        def _(slot=slot, kind=kind):
            if kind == "rope":
                v = _rope(p, rc_ref[...], ra_ref[...], rb_ref[...])
            elif kind == "rope_k":
                kk = _rope(p[:, :kw], rc_ref[:, :kw], ra_ref[:, :kw], rb_ref[:, :kw])
                nslot = 2 * G_NSA
                for s in range(nslot):
                    src = kk if s < G_NSA else p
                    outs[slot][pl.ds(s, p.shape[0], stride=nslot), :] = src[:, s * HEAD_DIM:(s + 1) * HEAD_DIM]
                return
            elif kind == "gate":
                v = jax.nn.sigmoid(p[:, :GATE_W])
            else:
                v = p
            outs[slot][...] = v.astype(outs[slot].dtype)


def _rope_tables(pos):
    inv = jnp.power(ROPE_THETA, -jnp.arange(ROT_HALF, dtype=F32) / ROT_HALF)
    ang = pos.astype(F32)[:, None] * inv[None, :]
    cos, sin = jnp.cos(ang), jnp.sin(ang)
    n = pos.shape[0]
    z16 = jnp.zeros((n, ROT_HALF), F32)
    tail0 = jnp.zeros((n, HEAD_DIM - 2 * ROT_HALF), F32)
    rc = jnp.concatenate([cos, cos, tail0 + 1.0], axis=1)
    ra = jnp.concatenate([z16, sin, tail0], axis=1)
    rb = jnp.concatenate([-sin, z16, tail0], axis=1)
    heads = IN_TN // HEAD_DIM
    return tuple(jnp.tile(t, (1, heads)) for t in (rc, ra, rb))


def in_projection(x2d, g1, shift, scale, w_bf, tables, tm, mod_map, tab_map):
    r, d = x2d.shape
    nslot = 2 * G_NSA
    wide = lambda lo, hi: ((r, (hi - lo) * IN_TN), (tm, IN_TN))
    flat = ((r * nslot, HEAD_DIM), (tm * nslot, HEAD_DIM))
    shapes = ([wide(*_IN_OUT_RANGES[k]) for k in range(3)] + [flat] * 3 + [((r, GATE_W), (tm, GATE_W))])
    dtypes = [BF16, F32, BF16, F32, F32, F32, F32]
    out_shape = [jax.ShapeDtypeStruct(s, dt) for (s, _), dt in zip(shapes, dtypes)]

    def out_map(lo, hi):
        return lambda i, j: (i, jnp.clip(j - lo, 0, hi - lo - 1))

    mod_block = (1,) + shift.shape[1:]
    return pl.pallas_call(
        _inproj_kernel,
        out_shape=out_shape,
        grid=(r // tm, IN_NT),
        in_specs=[pl.BlockSpec((tm, d), lambda i, j: (i, 0)),
                  pl.BlockSpec((1, d), lambda i, j: (0, 0)),
                  pl.BlockSpec(mod_block, lambda i, j: (mod_map(i), 0, 0)),
                  pl.BlockSpec(mod_block, lambda i, j: (mod_map(i), 0, 0)),
                  pl.BlockSpec((d, IN_TN), lambda i, j: (0, j)),
                  pl.BlockSpec((tm, IN_TN), lambda i, j: (tab_map(i), 0)),
                  pl.BlockSpec((tm, IN_TN), lambda i, j: (tab_map(i), 0)),
                  pl.BlockSpec((tm, IN_TN), lambda i, j: (tab_map(i), 0))],
        out_specs=[pl.BlockSpec(blk, out_map(lo, hi))
                   for (_, blk), (lo, hi) in zip(shapes, _IN_OUT_RANGES)],
        scratch_shapes=[pltpu.VMEM((tm, d), BF16)],
        compiler_params=_cparams(("arbitrary", "arbitrary")),
        name="in_projection",
    )(x2d, g1, shift, scale, w_bf, *tables)


def _sb_weights(z, valid, later, tri):
    rows, keys = z.shape
    sub = tri.shape[0]
    nsub = keys // sub
    sp = _softplus(z)
    log1m = -sp if valid is None else jnp.where(valid, -sp, 0.0)
    stacked = jnp.concatenate([log1m[:, i * sub:(i + 1) * sub] for i in range(nsub)], axis=0)
    hi, lo = _split_bf16(stacked)
    within = _dot(hi, tri) + _dot(lo, tri)
    afters = [None] * nsub
    for i in reversed(range(nsub)):
        w = within[i * rows:(i + 1) * rows]
        afters[i] = w + later
        later = later + w[:, 0:1] + log1m[:, i * sub:i * sub + 1]
    a = jnp.exp(z - sp + jnp.concatenate(afters, axis=1))
    if valid is not None:
        a = jnp.where(valid, a, 0.0)
    return a, later


def _later_keys_tri(sub):
    return (lax.broadcasted_iota(jnp.int32, (sub, sub), 0)
            > lax.broadcasted_iota(jnp.int32, (sub, sub), 1)).astype(BF16)


def _sb_prompt_kernel(q_ref, k_ref, v_ref, o_ref, *, tq, tk, nh):
    qi = pl.program_id(2)
    qpos = qi * tq + lax.broadcasted_iota(jnp.int32, (tq, tk), 0)
    lane = lax.broadcasted_iota(jnp.int32, (tq, tk), 1)
    tri = _later_keys_tri(min(MXU_N, tk))
    nk = ((qi + 1) * tq + tk - 1) // tk
    heads = [slice(hh * HEAD_DIM, (hh + 1) * HEAD_DIM) for hh in range(nh)]
    qs = [q_ref[:, cols] for cols in heads]

    def body(it, carry, masked):
        start = pl.multiple_of((nk - 1 - it) * tk, tk)
        valid = (start + lane) < qpos if masked else None
        out = []
        for (later, acc), q, cols in zip(carry, qs, heads):
            k = k_ref[pl.ds(start, tk), cols].astype(BF16)
            v = v_ref[pl.ds(start, tk), cols].astype(BF16)
            a, later = _sb_weights(_dot_nt(q, k) * SCALE, valid, later, tri)
            out.append((later, acc + _dot(a.astype(BF16), v)))
        return tuple(out)

    nfull = (qi * tq) // tk
    init = tuple((jnp.zeros((tq, 1), F32), jnp.zeros((tq, HEAD_DIM), F32)) for _ in heads)
    carry = lax.fori_loop(0, nk - nfull, functools.partial(body, masked=True), init)
    carry = lax.fori_loop(nk - nfull, nk, functools.partial(body, masked=False), carry)
    for (_, acc), cols in zip(carry, heads):
        o_ref[:, cols] = acc


def sb_prompt(q_sb, kv_sb, b, t):
    tq = 256
    tk = 512 if t % 512 == 0 else tq
    nq = t // tq
    nh = 2
    hw = nh * HEAD_DIM
    return pl.pallas_call(
        functools.partial(_sb_prompt_kernel, tq=tq, tk=tk, nh=nh),
        out_shape=jax.ShapeDtypeStruct((b * t, SB_W), F32),
        grid=(b, H_SB // nh, nq),
        in_specs=[pl.BlockSpec((tq, hw), lambda bi, h, qi: (bi * nq + qi, h)),
                  pl.BlockSpec((t, hw), lambda bi, h, qi: (bi, h)),
                  pl.BlockSpec((t, hw), lambda bi, h, qi: (bi, H_SB // nh + h))],
        out_specs=pl.BlockSpec((tq, hw), lambda bi, h, qi: (bi * nq + qi, h)),
        compiler_params=_cparams(("arbitrary", "arbitrary", "arbitrary")),
        name="sb_prompt",
    )(q_sb, kv_sb, kv_sb)


def _compress_rows(read_rows, pe_ref, w1_ref, w2_ref, x_scr, c):
    for p in range(CMP_BLK):
        x = jnp.concatenate([read_rows(p, 0), read_rows(p, 1)], axis=0) + pe_ref[c, p:p + 1, :]
        x_scr[:, p * HEAD_DIM:(p + 1) * HEAD_DIM] = x.astype(BF16)
    hid = _gelu(_dot(x_scr[...], w1_ref[c]))
    return _dot(hid.astype(BF16), w2_ref[c])


def _cmp_prompt_kernel(x_ref, pe_ref, w1_ref, w2_ref, o_ref, x_scr, *, nb):
    stride = CMP_BLK * 2 * G_NSA
    for c in range(2):
        def read_rows(p, g, c=c):
            return x_ref[0, pl.ds(p * 2 * G_NSA + c * G_NSA + g, nb, stride=stride), :]
        out = _compress_rows(read_rows, pe_ref, w1_ref, w2_ref, x_scr, c)
        o_ref[0, c, 0] = out[:nb]
        o_ref[0, c, 1] = out[nb:]


def compress_prompt(kv_cmp, pe, w1, w2, b, t):
    nb = t // CMP_BLK
    x = kv_cmp.reshape(b, t * 2 * G_NSA, HEAD_DIM)
    return pl.pallas_call(
        functools.partial(_cmp_prompt_kernel, nb=nb),
        out_shape=jax.ShapeDtypeStruct((b, 2, G_NSA, nb, HEAD_DIM), F32),
        grid=(b,),
        in_specs=[pl.BlockSpec((1, t * 2 * G_NSA, HEAD_DIM), lambda i: (i, 0, 0)),
                  pl.BlockSpec(pe.shape, lambda i: (0, 0, 0)),
                  pl.BlockSpec(w1.shape, lambda i: (0, 0, 0)),
                  pl.BlockSpec(w2.shape, lambda i: (0, 0, 0))],
        out_specs=pl.BlockSpec((1, 2, G_NSA, nb, HEAD_DIM), lambda i: (i, 0, 0, 0, 0)),
        scratch_shapes=[pltpu.VMEM((G_NSA * nb, CMP_BLK * HEAD_DIM), BF16)],
        compiler_params=_cparams(("arbitrary",)),
        name="compress_prompt",
    )(x, pe, w1, w2)


def _cmp_sample_kernel(pt_ref, *refs, pages, nbp, past):
    page_refs = refs[:pages]
    new_ref, pe_ref, w1_ref, w2_ref, o_ref, buf, x_scr = refs[pages:]
    b, jb = pl.program_id(0), pl.program_id(1)
    blk_rows = CMP_BLK * 2 * G_NSA
    blks_per_page = page_refs[0].shape[1] // blk_rows
    past_blks = past // CMP_BLK

    @pl.when((b == 0) & (jb == 0))
    def _():
        tail = buf.shape[0] - past_blks * CMP_PITCH
        buf[pl.ds(past_blks * CMP_PITCH, tail), :] = jnp.zeros((tail, HEAD_DIM), F32)

    for i, pr in enumerate(page_refs):
        for hb in range(blks_per_page):
            blk = (jb * pages + i) * blks_per_page + hb
            start = pl.multiple_of(blk * CMP_PITCH, SUBLANES)
            buf[pl.ds(start, blk_rows), :] = pr[0, hb * blk_rows:(hb + 1) * blk_rows, :]

    @pl.when(jb == pl.num_programs(1) - 1)
    def _():
        buf[pl.ds(past_blks * CMP_PITCH, new_ref.shape[1]), :] = new_ref[0]
        for c in range(2):
            def read_rows(p, g, c=c):
                return buf[pl.ds(p * 2 * G_NSA + c * G_NSA + g, nbp, stride=CMP_PITCH), :]
            out = _compress_rows(read_rows, pe_ref, w1_ref, w2_ref, x_scr, c)
            o_ref[0, c, 0] = out[:nbp]
            o_ref[0, c, 1] = out[nbp:]


def compress_sample(cache_cmp, page_table, kv_new, pe, w1, w2):
    n_pool, page = cache_cmp.shape[:2]
    bsz, n_pages = page_table.shape
    nq = kv_new.shape[1]
    past = n_pages * page
    nb = -(-(past + nq) // CMP_BLK)
    nbp = -(-nb // SUBLANES) * SUBLANES
    pages = min(PAGES_PER_STEP_NSA, n_pages)
    rpp = page * 2 * G_NSA
    cache = cache_cmp.reshape(n_pool, rpp, HEAD_DIM)
    new = kv_new.reshape(bsz, nq * 2 * G_NSA, HEAD_DIM)

    def page_map(i):
        return lambda b, jb, pt: (pt[b, jb * pages + i], 0, 0)

    grid_spec = pltpu.PrefetchScalarGridSpec(
        num_scalar_prefetch=1,
        grid=(bsz, n_pages // pages),
        in_specs=[pl.BlockSpec((1, rpp, HEAD_DIM), page_map(i)) for i in range(pages)]
        + [pl.BlockSpec((1,) + new.shape[1:], lambda b, jb, pt: (b, 0, 0)),
           pl.BlockSpec(pe.shape, lambda b, jb, pt: (0, 0, 0)),
           pl.BlockSpec(w1.shape, lambda b, jb, pt: (0, 0, 0)),
           pl.BlockSpec(w2.shape, lambda b, jb, pt: (0, 0, 0))],
        out_specs=pl.BlockSpec((1, 2, G_NSA, nbp, HEAD_DIM), lambda b, jb, pt: (b, 0, 0, 0, 0)),
        scratch_shapes=[pltpu.VMEM((nbp * CMP_PITCH, HEAD_DIM), F32),
                        pltpu.VMEM((G_NSA * nbp, CMP_BLK * HEAD_DIM), BF16)],
    )
    return pl.pallas_call(
        functools.partial(_cmp_sample_kernel, pages=pages, nbp=nbp, past=past),
        out_shape=jax.ShapeDtypeStruct((bsz, 2, G_NSA, nbp, HEAD_DIM), F32),
        grid_spec=grid_spec,
        compiler_params=_cparams(("arbitrary", "arbitrary")),
        name="compress_sample",
    )(page_table, *([cache] * pages), new, pe, w1, w2)


def _cmp_branch(q4, qpos4, kc, vc, nq, nbp):
    blk = lax.broadcasted_iota(jnp.int32, (1, nbp), 1)
    s = _dot_nt(q4, kc.astype(BF16)) * SCALE
    complete = (blk + 1) * CMP_BLK <= qpos4 + 1
    m = jnp.max(jnp.where(complete, s, NEG_INF), axis=1, keepdims=True)
    e = jnp.where(complete, jnp.exp(s - m), 0.0)
    den = jnp.sum(e, axis=1, keepdims=True)
    p = e / jnp.where(den > 0.0, den, 1.0)
    o_cmp = _dot(p.astype(BF16), vc.astype(BF16))
    imp = p[0:nq]
    for n in range(1, HPG):
        imp = imp + p[n * nq:(n + 1) * nq]
    qpos = qpos4[0:nq]
    cur = qpos // CMP_BLK
    started = blk <= cur
    forced = started & ((blk == 0) | (blk >= cur - 1))
    score = jnp.where(forced, FORCE_SCORE, jnp.where(started, imp, -FORCE_SCORE))
    rank = jnp.zeros((nq, nbp), F32)
    for j in range(nbp):
        col = score[:, j:j + 1]
        rank = rank + jnp.where(col > score, 1.0, 0.0) + jnp.where((col == score) & (blk > j), 1.0, 0.0)
    return o_cmp, rank


def _cmp_branch_t(q4, qpos_row, kc, vc, nq, nb):
    blk = lax.broadcasted_iota(jnp.int32, (nb, 1), 0)
    s = _dot_nt(kc.astype(BF16), q4) * SCALE
    complete = (blk + 1) * CMP_BLK <= qpos_row + 1
    m = jnp.max(jnp.where(complete, s, NEG_INF), axis=0, keepdims=True)
    e = jnp.where(complete, jnp.exp(s - m), 0.0)
    den = jnp.sum(e, axis=0, keepdims=True)
    p = e / jnp.where(den > 0.0, den, 1.0)
    o_cmp = _dot_tn(p.astype(BF16), vc.astype(BF16))
    imp = p[:, 0:nq]
    for n in range(1, HPG):
        imp = imp + p[:, n * nq:(n + 1) * nq]
    cur = qpos_row[:, 0:nq] // CMP_BLK
    started = blk <= cur
    forced = started & ((blk == 0) | (blk >= cur - 1))
    score = jnp.where(forced, FORCE_SCORE, jnp.where(started, imp, -FORCE_SCORE))
    rank = jnp.zeros((nb, nq), F32)
    for j in range(nb):
        row = score[j:j + 1, :]
        rank = rank + jnp.where(row > score, 1.0, 0.0) + jnp.where((row == score) & (blk > j), 1.0, 0.0)
    return o_cmp, rank


def _flash_tile(q4, k, v, mask, state):
    m, l, acc = state
    s = _dot_nt(q4, k) * SCALE
    m_new = jnp.maximum(m, jnp.max(jnp.where(mask, s, NEG_INF), axis=1, keepdims=True))
    p = jnp.where(mask, jnp.exp(s - m_new), 0.0)
    alpha = jnp.exp(m - m_new)
    l = alpha * l + jnp.sum(p, axis=1, keepdims=True)
    acc = alpha * acc + _dot(p.astype(BF16), v)
    return m_new, l, acc


def _flash_init(rows):
    return (jnp.full((rows, 1), NEG_INF, F32), jnp.zeros((rows, 1), F32),
            jnp.zeros((rows, HEAD_DIM), F32))


def _flash_out(state):
    _, l, acc = state
    return acc / jnp.where(l > 0.0, l, 1.0)


def _stack_heads(q_ref):
    return jnp.concatenate([q_ref[:, n * HEAD_DIM:(n + 1) * HEAD_DIM] for n in range(HPG)], axis=0)


def _nsa_prompt_kernel(q_ref, kc_ref, vc_ref, slc_ref, win_ref, gate_ref, o_ref,
                       selx_scr, *, tq, tk, nb, topk):
    g, qi = pl.program_id(1), pl.program_id(2)
    rows = HPG * tq
    nslot = 2 * G_NSA
    t = slc_ref.shape[0] // nslot
    q4 = _stack_heads(q_ref)

    def keys(ref, start, n):
        return (ref[pl.ds(start * nslot + g, n, stride=nslot), :].astype(BF16),
                ref[pl.ds(start * nslot + G_NSA + g, n, stride=nslot), :].astype(BF16))
    t_in = lax.broadcasted_iota(jnp.int32, (tq, 1), 0)
    qpos4 = jnp.concatenate([qi * tq + t_in] * HPG, axis=0)
    qpos_row = qi * tq + lax.broadcasted_iota(jnp.int32, (1, rows), 1) % tq
    o_cmp, rank_t = _cmp_branch_t(q4, qpos_row, kc_ref[0, 0, 0], vc_ref[0, 0, 0], tq, nb)
    sel_t = jnp.where(rank_t < topk, 1.0, 0.0).astype(BF16)
    kb = lax.broadcasted_iota(jnp.int32, (nb, t), 0)
    kl = lax.broadcasted_iota(jnp.int32, (nb, t), 1)
    selx = _dot_tn(sel_t, jnp.where(kl // CMP_BLK == kb, 1.0, 0.0).astype(BF16))
    for j in range(t // tk):
        selx_scr[j] = selx[:, j * tk:(j + 1) * tk]
    lane = lax.broadcasted_iota(jnp.int32, (rows, tk), 1)

    def slc_body(j, state):
        start = pl.multiple_of(j * tk, tk)
        k, v = keys(slc_ref, start, tk)
        sx = selx_scr[j]
        mask = (jnp.concatenate([sx] * HPG, axis=0) > 0.5) & ((start + lane) <= qpos4)
        return _flash_tile(q4, k, v, mask, state)

    nk = ((qi + 1) * tq + tk - 1) // tk
    o_slc = _flash_out(lax.fori_loop(0, nk, slc_body, _flash_init(rows)))

    wlen = min(WINDOW + tq, t)
    start = pl.multiple_of(jnp.clip(qi * tq - WINDOW, 0, t - wlen), tq)
    d = qpos4 - (start + lax.broadcasted_iota(jnp.int32, (rows, wlen), 1))
    o_win = _flash_out(_flash_tile(q4, *keys(win_ref, start, wlen), (d >= 0) & (d < WINDOW),
                                   _flash_init(rows)))

    gates = gate_ref[...]

    def gate(n, branch):
        col = gates[:, n * 3 + branch:n * 3 + branch + 1]
        for gg in range(1, G_NSA):
            c = (gg * HPG + n) * 3 + branch
            col = jnp.where(g == gg, gates[:, c:c + 1], col)
        return col

    for n in range(HPG):
        r0 = n * tq
        o_ref[:, n * HEAD_DIM:(n + 1) * HEAD_DIM] = (
            gate(n, 0) * o_cmp[r0:r0 + tq] + gate(n, 1) * o_slc[r0:r0 + tq]
            + gate(n, 2) * o_win[r0:r0 + tq])


def nsa_prompt(q_nsa, kvc, kv_slc, kv_win, gates, b, t):
    tq = 128
    tk = 512 if t % 512 == 0 else tq
    nq = t // tq
    nb = t // CMP_BLK
    topk = min(TOP_BLOCKS, nb)
    gw = HPG * HEAD_DIM
    return pl.pallas_call(
        functools.partial(_nsa_prompt_kernel, tq=tq, tk=tk, nb=nb, topk=topk),
        out_shape=jax.ShapeDtypeStruct((b * t, NSA_W), F32),
        grid=(b, G_NSA, nq),
        in_specs=[pl.BlockSpec((tq, gw), lambda bi, g, qi: (bi * nq + qi, g)),
                  pl.BlockSpec((1, 1, 1, nb, HEAD_DIM), lambda bi, g, qi: (bi, 0, g, 0, 0)),
                  pl.BlockSpec((1, 1, 1, nb, HEAD_DIM), lambda bi, g, qi: (bi, 1, g, 0, 0)),
                  pl.BlockSpec((t * 2 * G_NSA, HEAD_DIM), lambda bi, g, qi: (bi, 0)),
                  pl.BlockSpec((t * 2 * G_NSA, HEAD_DIM), lambda bi, g, qi: (bi, 0)),
                  pl.BlockSpec((tq, GATE_W), lambda bi, g, qi: (bi * nq + qi, 0))],
        out_specs=pl.BlockSpec((tq, gw), lambda bi, g, qi: (bi * nq + qi, g)),
        scratch_shapes=[pltpu.VMEM((t // tk, tq, tk), F32)],
        compiler_params=_cparams(("arbitrary", "arbitrary", "arbitrary")),
        name="nsa_prompt",
    )(q_nsa, kvc, kvc, kv_slc, kv_win, gates)


def _sb_sample_kernel(pt_ref, *refs, pages, nq):
    k_refs, v_refs = refs[:pages], refs[pages:2 * pages]
    qt_ref, new_ref, o_ref, kbuf, vbuf, acc_scr, later_scr = refs[2 * pages:]
    jb = pl.program_id(1)
    page = new_ref.shape[1]
    rows = H_SB * nq
    tri = _later_keys_tri(LANES)
    qt = qt_ref[0]

    def sweep(k, v, valid):
        z = _dot_nt(qt, k) * SCALE
        a, later = _sb_weights(z, valid, later_scr[...], tri)
        acc_scr[...] += _dot(a.astype(BF16), v)
        later_scr[...] = later

    @pl.when(jb == 0)
    def _():
        acc_scr[...] = jnp.zeros_like(acc_scr)
        later_scr[...] = jnp.zeros_like(later_scr)
        key = lax.broadcasted_iota(jnp.int32, (rows, page), 1)
        qry = lax.broadcasted_iota(jnp.int32, (rows, page), 0) % nq
        sweep(new_ref[0, :, :SB_W].astype(BF16), new_ref[0, :, SB_W:].astype(BF16), key < qry)

    for src_refs, buf in ((k_refs, kbuf), (v_refs, vbuf)):
        for i, pr in enumerate(src_refs):
            by_head = jnp.swapaxes(pr[0, :, 0], 0, 1).astype(BF16)
            for h in range(H_SB):
                buf[i * page:(i + 1) * page, h * HEAD_DIM:(h + 1) * HEAD_DIM] = by_head[h]
    sweep(kbuf[...], vbuf[...], None)

    @pl.when(jb == pl.num_programs(1) - 1)
    def _():
        acc = acc_scr[...]
        for h in range(H_SB):
            o_ref[0, :, h * HEAD_DIM:(h + 1) * HEAD_DIM] = acc[h * nq:(h + 1) * nq,
                                                               h * HEAD_DIM:(h + 1) * HEAD_DIM]


def sb_sample(cache_sb, page_table, q_sb, kv_new):
    n_pool, page = cache_sb.shape[:2]
    bsz, n_pages = page_table.shape
    nq = q_sb.shape[1]
    pages = min(PAGES_PER_STEP_SB, n_pages)
    nsteps = n_pages // pages
    half = (1, page, 1, H_SB, HEAD_DIM)
    qh = q_sb.reshape(bsz, nq, H_SB, HEAD_DIM).transpose(0, 2, 1, 3)
    qt = (qh[:, :, :, None, :] * jnp.eye(H_SB, dtype=BF16)[None, :, None, :, None])
    qt = qt.reshape(bsz, H_SB * nq, SB_W)
    new_page = jnp.pad(kv_new, ((0, 0), (0, page - nq), (0, 0)))

    def page_map(i, c):
        return lambda b, jb, pt: (pt[b, n_pages - (jb + 1) * pages + i], 0, c, 0, 0)

    grid_spec = pltpu.PrefetchScalarGridSpec(
        num_scalar_prefetch=1,
        grid=(bsz, nsteps),
        in_specs=[pl.BlockSpec(half, page_map(i, c)) for c in range(2) for i in range(pages)]
        + [pl.BlockSpec((1, H_SB * nq, SB_W), lambda b, jb, pt: (b, 0, 0)),
           pl.BlockSpec((1, page, 2 * SB_W), lambda b, jb, pt: (b, 0, 0))],
        out_specs=pl.BlockSpec((1, nq, SB_W), lambda b, jb, pt: (b, 0, 0)),
        scratch_shapes=[pltpu.VMEM((pages * page, SB_W), BF16), pltpu.VMEM((pages * page, SB_W), BF16),
                        pltpu.VMEM((H_SB * nq, SB_W), F32), pltpu.VMEM((H_SB * nq, 1), F32)],
    )
    return pl.pallas_call(
        functools.partial(_sb_sample_kernel, pages=pages, nq=nq),
        out_shape=jax.ShapeDtypeStruct((bsz, nq, SB_W), F32),
        grid_spec=grid_spec,
        compiler_params=_cparams(("arbitrary", "arbitrary")),
        name="sb_sample",
    )(page_table, *([cache_sb] * (2 * pages)), qt, new_page)


def _nsa_sample_sel_kernel(q_ref, kc_ref, vc_ref, st_ref, new_ref, ocmp_ref, owin_ref, selx_ref,
                           *, nq, nbp, past, topk):
    rows = HPG * nq
    slots = 2 * G_NSA
    wb = st_ref.shape[1] // slots
    t_in = lax.broadcasted_iota(jnp.int32, (nq, 1), 0)
    qpos4 = jnp.concatenate([past + t_in] * HPG, axis=0)
    kb = lax.broadcasted_iota(jnp.int32, (nbp, nbp * CMP_BLK), 0)
    kl = lax.broadcasted_iota(jnp.int32, (nbp, nbp * CMP_BLK), 1)
    expand = jnp.where(kl // CMP_BLK == kb, 1.0, 0.0).astype(BF16)
    lane_w = lax.broadcasted_iota(jnp.int32, (rows, wb), 1)
    lane_n = lax.broadcasted_iota(jnp.int32, (rows, new_ref.shape[1]), 1)
    for g in range(G_NSA):
        q4 = jnp.concatenate([q_ref[0, :, (g * HPG + n) * HEAD_DIM:(g * HPG + n + 1) * HEAD_DIM]
                              for n in range(HPG)], axis=0)
        o_cmp, rank = _cmp_branch(q4, qpos4, kc_ref[0, 0, g], vc_ref[0, 0, g], nq, nbp)
        sel = jnp.where(rank < topk, 1.0, 0.0).astype(BF16)
        selx_ref[0, g] = _dot(sel, expand)
        kcol, vcol = g * HEAD_DIM, (G_NSA + g) * HEAD_DIM
        state = _flash_init(rows)
        d = qpos4 - (past - wb + lane_w)
        state = _flash_tile(q4, st_ref[0, pl.ds(g, wb, stride=slots), :].astype(BF16),
                            st_ref[0, pl.ds(G_NSA + g, wb, stride=slots), :].astype(BF16),
                            (d >= 0) & (d < WINDOW), state)
        d = qpos4 - (past + lane_n)
        state = _flash_tile(q4, new_ref[0, :, kcol:kcol + HEAD_DIM].astype(BF16),
                            new_ref[0, :, vcol:vcol + HEAD_DIM].astype(BF16),
                            (d >= 0) & (d < WINDOW) & (lane_n < nq), state)
        o_win = _flash_out(state)
        for n in range(HPG):
            h = g * HPG + n
            ocmp_ref[0, :, h * HEAD_DIM:(h + 1) * HEAD_DIM] = o_cmp[n * nq:(n + 1) * nq]
            owin_ref[0, :, h * HEAD_DIM:(h + 1) * HEAD_DIM] = o_win[n * nq:(n + 1) * nq]


def nsa_sample_select(q_nsa, kvc, state_win, win_new, past):
    bsz, nq = q_nsa.shape[:2]
    nbp = kvc.shape[3]
    nb = -(-(past + nq) // CMP_BLK)
    wrows = state_win.shape[1]
    return pl.pallas_call(
        functools.partial(_nsa_sample_sel_kernel, nq=nq, nbp=nbp, past=past, topk=min(TOP_BLOCKS, nb)),
        out_shape=[jax.ShapeDtypeStruct((bsz, nq, NSA_W), F32),
                   jax.ShapeDtypeStruct((bsz, nq, NSA_W), F32),
                   jax.ShapeDtypeStruct((bsz, G_NSA, nq, nbp * CMP_BLK), F32)],
        grid=(bsz,),
        in_specs=[pl.BlockSpec((1, nq, NSA_W), lambda b: (b, 0, 0)),
                  pl.BlockSpec((1, 1, G_NSA, nbp, HEAD_DIM), lambda b: (b, 0, 0, 0, 0)),
                  pl.BlockSpec((1, 1, G_NSA, nbp, HEAD_DIM), lambda b: (b, 1, 0, 0, 0)),
                  pl.BlockSpec((1, wrows, HEAD_DIM), lambda b: (b, 0, 0)),
                  pl.BlockSpec((1,) + win_new.shape[1:], lambda b: (b, 0, 0))],
        out_specs=[pl.BlockSpec((1, nq, NSA_W), lambda b: (b, 0, 0)),
                   pl.BlockSpec((1, nq, NSA_W), lambda b: (b, 0, 0)),
                   pl.BlockSpec((1, G_NSA, nq, nbp * CMP_BLK), lambda b: (b, 0, 0, 0))],
        compiler_params=_cparams(("arbitrary",)),
        name="nsa_sample_select",
    )(q_nsa, kvc, kvc, state_win, win_new)


def _nsa_sample_slc_kernel(pt_ref, *refs, pages, nq, past):
    page_refs = refs[:pages]
    (q_ref, new_ref, selx_ref, selx_new_ref, ocmp_ref, owin_ref, gate_ref, o_ref,
     kbuf, vbuf, m_scr, l_scr, acc_scr) = refs[pages:]
    jb = pl.program_id(1)
    rows = HPG * nq
    slots = 2 * G_NSA
    page = page_refs[0].shape[1] // slots
    t_in = lax.broadcasted_iota(jnp.int32, (nq, 1), 0)
    qpos4 = jnp.concatenate([past + t_in] * HPG, axis=0)
    q4s = [jnp.concatenate([q_ref[0, :, (g * HPG + n) * HEAD_DIM:(g * HPG + n + 1) * HEAD_DIM]
                            for n in range(HPG)], axis=0) for g in range(G_NSA)]

    @pl.when(jb == 0)
    def _():
        m_scr[...] = jnp.full(m_scr.shape, NEG_INF, F32)
        l_scr[...] = jnp.zeros_like(l_scr)
        acc_scr[...] = jnp.zeros_like(acc_scr)

    def sweep(g, k, v, sx, causal):
        mask = jnp.concatenate([sx] * HPG, axis=0) > 0.5
        if causal is not None:
            mask = mask & causal
        state = (m_scr[g], l_scr[g], acc_scr[g])
        m_scr[g], l_scr[g], acc_scr[g] = _flash_tile(q4s[g], k, v, mask, state)

    for g in range(G_NSA):
        for i, pr in enumerate(page_refs):
            dst = slice(i * page, (i + 1) * page)
            kbuf[g, dst, :] = pr[0, pl.ds(g, page, stride=slots), :].astype(BF16)
            vbuf[g, dst, :] = pr[0, pl.ds(G_NSA + g, page, stride=slots), :].astype(BF16)
        sweep(g, kbuf[g], vbuf[g], selx_ref[0, g], None)

    @pl.when(jb == pl.num_programs(1) - 1)
    def _():
        nrow = new_ref.shape[1]
        lane_n = lax.broadcasted_iota(jnp.int32, (rows, nrow), 1)
        for g in range(G_NSA):
            kcol, vcol = g * HEAD_DIM, (G_NSA + g) * HEAD_DIM
            sweep(g, new_ref[0, :, kcol:kcol + HEAD_DIM].astype(BF16),
                  new_ref[0, :, vcol:vcol + HEAD_DIM].astype(BF16),
                  selx_new_ref[0, g, :, 0:nrow], ((past + lane_n) <= qpos4) & (lane_n < nq))
        gates = gate_ref[0]
        for g in range(G_NSA):
            o_slc = _flash_out((m_scr[g], l_scr[g], acc_scr[g]))
            for n in range(HPG):
                h = g * HPG + n
                cols = slice(h * HEAD_DIM, (h + 1) * HEAD_DIM)
                o_ref[0, :, cols] = (gates[:, 3 * h:3 * h + 1] * ocmp_ref[0, :, cols]
                                     + gates[:, 3 * h + 1:3 * h + 2] * o_slc[n * nq:(n + 1) * nq]
                                     + gates[:, 3 * h + 2:3 * h + 3] * owin_ref[0, :, cols])


def nsa_sample_slc(cache_slc, page_table, q_nsa, slc_new, selx, o_cmp, o_win, gates):
    n_pool, page = cache_slc.shape[:2]
    bsz, n_pages = page_table.shape
    nq = q_nsa.shape[1]
    past = n_pages * page
    pages = min(PAGES_PER_STEP_SLC, n_pages)
    prow = page * 2 * G_NSA
    cache = cache_slc.reshape(n_pool, prow, HEAD_DIM)
    rows = HPG * nq

    def page_map(i):
        return lambda b, jb, pt: (pt[b, jb * pages + i], 0, 0)

    per_b = lambda b, jb, pt: (b, 0, 0)
    grid_spec = pltpu.PrefetchScalarGridSpec(
        num_scalar_prefetch=1,
        grid=(bsz, n_pages // pages),
        in_specs=[pl.BlockSpec((1, prow, HEAD_DIM), page_map(i)) for i in range(pages)]
        + [pl.BlockSpec((1, nq, NSA_W), per_b),
           pl.BlockSpec((1,) + slc_new.shape[1:], per_b),
           pl.BlockSpec((1, G_NSA, nq, pages * page), lambda b, jb, pt: (b, 0, 0, jb)),
           pl.BlockSpec((1, G_NSA, nq, LANES), lambda b, jb, pt: (b, 0, 0, past // LANES)),
           pl.BlockSpec((1, nq, NSA_W), per_b),
           pl.BlockSpec((1, nq, NSA_W), per_b),
           pl.BlockSpec((1, nq, GATE_W), per_b)],
        out_specs=pl.BlockSpec((1, nq, NSA_W), per_b),
        scratch_shapes=[pltpu.VMEM((G_NSA, pages * page, HEAD_DIM), BF16),
                        pltpu.VMEM((G_NSA, pages * page, HEAD_DIM), BF16),
                        pltpu.VMEM((G_NSA, rows, 1), F32), pltpu.VMEM((G_NSA, rows, 1), F32),
                        pltpu.VMEM((G_NSA, rows, HEAD_DIM), F32)],
    )
    return pl.pallas_call(
        functools.partial(_nsa_sample_slc_kernel, pages=pages, nq=nq, past=past),
        out_shape=jax.ShapeDtypeStruct((bsz, nq, NSA_W), F32),
        grid_spec=grid_spec,
        compiler_params=_cparams(("arbitrary", "arbitrary")),
        name="nsa_sample_slc",
    )(page_table, *([cache] * pages), q_nsa, slc_new, selx, selx, o_cmp, o_win, gates)


def _outproj_kernel(osb_ref, onsa_ref, gsb_ref, gnsa_ref, wa_ref, wb_ref, x_ref, gate_ref,
                    g2_ref, shift_ref, scale_ref, x1_ref, h2_ref):
    a = _rms(osb_ref[...], gsb_ref[...]).astype(BF16)
    b = _rms(onsa_ref[...], gnsa_ref[...]).astype(BF16)
    mixed = _dot(a, wa_ref[...]) + _dot(b, wb_ref[...])
    x1 = x_ref[...] + gate_ref[0] * mixed
    x1_ref[...] = x1
    h2_ref[...] = (_rms(x1, g2_ref[...]) * (1.0 + scale_ref[0]) + shift_ref[0]).astype(BF16)


def out_projection(o_sb, o_nsa, g_sb, g_nsa, w_out_bf, x2d, gate, g2, shift, scale, tm, mod_map):
    r, d = x2d.shape
    mod_block = (1,) + gate.shape[1:]
    mod_spec = pl.BlockSpec(mod_block, lambda i: (mod_map(i), 0, 0))
    row = lambda w: pl.BlockSpec((tm, w), lambda i: (i, 0))
    const = lambda shp: pl.BlockSpec(shp, lambda i: (0,) * len(shp))
    return pl.pallas_call(
        _outproj_kernel,
        out_shape=[jax.ShapeDtypeStruct((r, d), F32), jax.ShapeDtypeStruct((r, d), BF16)],
        grid=(r // tm,),
        in_specs=[row(SB_W), row(NSA_W), const((1, SB_W)), const((1, NSA_W)),
                  pl.BlockSpec((SB_W, d), lambda i: (0, 0)),
                  pl.BlockSpec((NSA_W, d), lambda i: (1, 0)),
                  row(d), mod_spec, const((1, d)), mod_spec, mod_spec],
        out_specs=[row(d), row(d)],
        compiler_params=_cparams(("arbitrary",)),
        name="out_projection",
    )(o_sb, o_nsa, g_sb, g_nsa, w_out_bf, w_out_bf, x2d, gate, g2, shift, scale)


def _peer_score_kernel(h_ref, wq_ref, k1_ref, k2_ref, s_ref):
    q = _dot(h_ref[...], wq_ref[...]).astype(BF16)
    half = D_QUERY // 2
    nchunk = h_ref.shape[0] // LANES
    for h in range(PEER_HEADS):
        for c, k_ref in enumerate((k1_ref, k2_ref)):
            qh = q[:, h * D_QUERY + c * half:h * D_QUERY + (c + 1) * half]
            st = _dot_nt(k_ref[h], qh)
            for ch in range(nchunk):
                s_ref[c, h, ch] = st[:, ch * LANES:(ch + 1) * LANES]


def peer_scores(h2, wq_bf, k1_bf, k2_bf):
    r, d = h2.shape
    tm = PEER_TN
    nchunk = tm // LANES
    return pl.pallas_call(
        _peer_score_kernel,
        out_shape=jax.ShapeDtypeStruct((2, PEER_HEADS, r // LANES, N_KEYS, LANES), F32),
        grid=(r // tm,),
        in_specs=[pl.BlockSpec((tm, d), lambda i: (i, 0)),
                  pl.BlockSpec(wq_bf.shape, lambda i: (0, 0)),
                  pl.BlockSpec(k1_bf.shape, lambda i: (0, 0, 0)),
                  pl.BlockSpec(k2_bf.shape, lambda i: (0, 0, 0))],
        out_specs=pl.BlockSpec((2, PEER_HEADS, nchunk, N_KEYS, LANES), lambda i: (0, 0, i, 0, 0)),
        compiler_params=_cparams(("arbitrary",)),
        name="peer_scores",
    )(h2, wq_bf, k1_bf, k2_bf)


def _top_values(x, n, scr, with_rank):
    rank = jnp.full(x.shape, float(n), F32) if with_rank else None
    for it in range(n):
        m = jnp.max(x, axis=0, keepdims=True)
        scr[it:it + 1, :] = m
        hit = x == m
        if with_rank:
            rank = jnp.where(hit, float(it), rank)
        x = jnp.where(hit, NEG_INF, x)
    return scr[0:n, :], rank


def _pack_bf16(x):
    return pltpu.bitcast(x.astype(BF16), jnp.int32)


def _peer_select_kernel(s_ref, r2_ref, e2_ref, cnt_ref, e1_ref, v1_scr, v2_scr):
    k = PEER_TOPK
    nchunk = s_ref.shape[2]

    def body(idx, _):
        h, ch = idx // nchunk, idx % nchunk
        s1 = s_ref[0, h, ch]
        s2 = s_ref[1, h, ch]
        v1, _ = _top_values(s1, k, v1_scr, False)
        v2, r2 = _top_values(s2, k, v2_scr, True)
        x = jnp.concatenate([v1[0:1] + v2] + [v1[i:i + 1] + v2[0:k // 2] for i in range(1, k // 2)]
                            + [v1[k // 2:k] + v2[0:1]], axis=0)
        xs = x
        for _ in range(k - 1):
            xs = jnp.where(xs == jnp.max(xs, axis=0, keepdims=True), NEG_INF, xs)
        tau = jnp.max(xs, axis=0, keepdims=True)
        vmax = v1[0:1] + v2[0:1]
        chosen = x >= tau
        z = jnp.sum(jnp.where(chosen, jnp.exp(x - vmax), 0.0), axis=0, keepdims=True)
        c = jnp.where(chosen, 1.0, 0.0)
        cnt_rows = ([jnp.sum(c[0:k], axis=0, keepdims=True)]
                    + [jnp.sum(c[k + (k // 2) * (i - 1):k + (k // 2) * i], axis=0, keepdims=True)
                       for i in range(1, k // 2)])
        cnt_tab = jnp.concatenate(cnt_rows + [c[k + (k // 2) * (k // 2 - 1):]], axis=0)
        cnt = jnp.zeros_like(s1)
        for i in range(k):
            cnt = jnp.where(s1 == v1[i:i + 1], cnt_tab[i:i + 1], cnt)
        r2_ref[h, ch] = _pack_bf16(r2)
        e2_ref[h, ch] = _pack_bf16(jnp.exp(s2 - v2[0:1]))
        cnt_ref[h, ch] = cnt
        e1_ref[h, ch] = jnp.exp(s1 - v1[0:1]) / z
        return 0

    lax.fori_loop(0, PEER_HEADS * nchunk, body, 0)


def peer_select(s):
    _, heads, nch_all, keys, lanes = s.shape
    nchunk = PEER_TN // LANES
    blk = (heads, nchunk, keys, lanes)
    pblk = (heads, nchunk, keys // 2, lanes)
    full = lambda shp: (shp[0], nch_all) + shp[2:]
    return pl.pallas_call(
        _peer_select_kernel,
        out_shape=[jax.ShapeDtypeStruct(full(pblk), jnp.int32),
                   jax.ShapeDtypeStruct(full(pblk), jnp.int32),
                   jax.ShapeDtypeStruct(full(blk), F32),
                   jax.ShapeDtypeStruct(full(blk), F32)],
        grid=(nch_all // nchunk,),
        in_specs=[pl.BlockSpec((2,) + blk, lambda i: (0, 0, i, 0, 0))],
        out_specs=[pl.BlockSpec(pblk, lambda i: (0, i, 0, 0)),
                   pl.BlockSpec(pblk, lambda i: (0, i, 0, 0)),
                   pl.BlockSpec(blk, lambda i: (0, i, 0, 0)),
                   pl.BlockSpec(blk, lambda i: (0, i, 0, 0))],
        scratch_shapes=[pltpu.VMEM((PEER_TOPK, LANES), F32), pltpu.VMEM((PEER_TOPK, LANES), F32)],
        compiler_params=_cparams(("arbitrary",)),
        name="peer_select",
    )(s)


def _row_bf16(row):
    packed_rows = 2 * SUBLANES
    return pltpu.repeat(jnp.broadcast_to(row, (packed_rows, LANES)).astype(BF16), N_KEYS // packed_rows, axis=0)


def _peer_expert_kernel(h_ref, u_ref, vt_ref, r2_ref, e2_ref, cnt_ref, e1_ref, o_ref, ga0, ga1, g0, g1):
    e = pl.program_id(1)
    ne = pl.num_programs(1) - 2
    nchunk = h_ref.shape[0] // LANES

    @pl.when(e == 0)
    def _():
        o_ref[...] = jnp.zeros_like(o_ref)
        ga0[...] = jnp.zeros_like(ga0)
        g1[...] = jnp.zeros_like(g1)

    def step(g_cur, g_prev, ga_prev, ga_prev2):
        et = jnp.minimum(e, ne - 1)
        zero = jnp.zeros((N_KEYS, LANES), BF16)
        for ch in range(nchunk):
            for al in range(PEER_AT):
                a = et * PEER_AT + al
                gsum = zero
                for h in range(PEER_HEADS):
                    hit = pltpu.bitcast(r2_ref[h, ch], BF16) < _row_bf16(cnt_ref[h, ch, pl.ds(a, 1), :])
                    e2 = pltpu.bitcast(e2_ref[h, ch], BF16)
                    gsum = gsum + jnp.where(hit, e2, zero) * _row_bf16(e1_ref[h, ch, pl.ds(a, 1), :])
                g_cur[al * N_KEYS:(al + 1) * N_KEYS, ch * LANES:(ch + 1) * LANES] = gsum
        o_ref[...] += _dot(vt_ref[...], ga_prev2[...])
        act = _gelu(_dot_nt(u_ref[...], h_ref[...]))
        ga_prev[...] = act.astype(BF16) * g_prev[...]

    @pl.when(e % 2 == 0)
    def _():
        step(g0, g1, ga1, ga0)

    @pl.when(e % 2 == 1)
    def _():
        step(g1, g0, ga0, ga1)


def peer_experts(h2, u_bf, vt_bf, r2p, e2p, cnt, e1):
    r, d = h2.shape
    n_exp = u_bf.shape[0]
    tn = PEER_TN
    nchunk = tn // LANES
    et = PEER_AT * N_KEYS
    sblk = (PEER_HEADS, nchunk, N_KEYS, LANES)
    pblk = (PEER_HEADS, nchunk, N_KEYS // 2, LANES)
    ne = n_exp // et
    return pl.pallas_call(
        _peer_expert_kernel,
        out_shape=jax.ShapeDtypeStruct((d, r), F32),
        grid=(r // tn, ne + 2),
        in_specs=[pl.BlockSpec((tn, d), lambda t, e: (t, 0)),
                  pl.BlockSpec((et, d), lambda t, e: (jnp.clip(e - 1, 0, ne - 1), 0)),
                  pl.BlockSpec((d, et), lambda t, e: (0, jnp.clip(e - 2, 0, ne - 1))),
                  pl.BlockSpec(pblk, lambda t, e: (0, t, 0, 0)),
                  pl.BlockSpec(pblk, lambda t, e: (0, t, 0, 0)),
                  pl.BlockSpec(sblk, lambda t, e: (0, t, 0, 0)),
                  pl.BlockSpec(sblk, lambda t, e: (0, t, 0, 0))],
        out_specs=pl.BlockSpec((d, tn), lambda t, e: (0, t)),
        scratch_shapes=[pltpu.VMEM((et, tn), BF16), pltpu.VMEM((et, tn), BF16),
                        pltpu.VMEM((et, tn), BF16), pltpu.VMEM((et, tn), BF16)],
        compiler_params=_cparams(("arbitrary", "arbitrary")),
        name="peer_experts",
    )(h2, u_bf, vt_bf, r2p, e2p, cnt, e1)


def _final_kernel(x1_ref, pt_ref, gate_ref, g_ref, y_ref):
    x2 = x1_ref[...] + gate_ref[0] * pt_ref[...].T
    y_ref[...] = _rms(x2, g_ref[...])


def final_norm(x1, peer_t, col0, gate, g, tm, mod_map):
    r, d = x1.shape
    mod_block = (1,) + gate.shape[1:]
    return pl.pallas_call(
        _final_kernel,
        out_shape=jax.ShapeDtypeStruct((r, d), F32),
        grid=(r // tm,),
        in_specs=[pl.BlockSpec((tm, d), lambda i: (i, 0)),
                  pl.BlockSpec((d, tm), lambda i: (0, col0 // tm + i)),
                  pl.BlockSpec(mod_block, lambda i: (mod_map(i), 0, 0)),
                  pl.BlockSpec((1, d), lambda i: (0, 0))],
        out_specs=pl.BlockSpec((tm, d), lambda i: (i, 0)),
        compiler_params=_cparams(("arbitrary",)),
        name="final_norm",
    )(x1, peer_t, gate, g)


def _layer(x_prompt, x_sample, cache_sb, cache_cmp, cache_slc, state_win, page_table, c_prompt,
           c_sample, w_ada, b_ada, norm1_g, w_in, cmp_pe_k, cmp_w1_k, cmp_w2_k, cmp_pe_v, cmp_w1_v,
           cmp_w2_v, out_g_sb, out_g_nsa, w_out, norm2_g, peer_wq, peer_k1, peer_k2, peer_u, peer_v):
    b, t, d = x_prompt.shape
    bs, nq, _ = x_sample.shape
    n_pages, page = page_table.shape[1], cache_sb.shape[1]
    past = n_pages * page
    rp, rs = b * t, bs * nq

    w_in_bf = jnp.concatenate([w_in.astype(BF16), jnp.zeros((d, IN_NT * IN_TN - D_IN), BF16)], axis=1)
    w_out_bf = w_out.astype(BF16)
    wq_bf = peer_wq.astype(BF16)
    k1_bf, k2_bf = peer_k1.astype(BF16), peer_k2.astype(BF16)
    u_bf = peer_u.astype(BF16)
    vt_bf = peer_v.astype(BF16).T
    pe = jnp.stack([cmp_pe_k, cmp_pe_v])
    w1 = jnp.stack([cmp_w1_k, cmp_w1_v]).astype(BF16).reshape(2, CMP_BLK * HEAD_DIM, HEAD_DIM)
    w2 = jnp.stack([cmp_w2_k, cmp_w2_v]).astype(BF16)
    row = lambda v: v.reshape(1, -1)

    n_c = b + bs
    c_all = jnp.pad(jnp.concatenate([c_prompt, c_sample]), ((0, -n_c % 16), (0, 0))).astype(BF16)
    mod = adaln_mod(c_all, w_ada, row(b_ada))
    mod_p = [mod[:b, k * d:(k + 1) * d].reshape(b, 1, d) for k in range(6)]
    mod_s = [jnp.repeat(mod[b:n_c, k * d:(k + 1) * d], nq, axis=0).reshape(1, rs, d) for k in range(6)]

    tm_p = 512 if t % 512 == 0 else t
    xp = x_prompt.reshape(rp, d)
    tabs_p = _rope_tables(jnp.arange(t))
    qsb_p, kvsb_p, qnsa_p, kvcmp_p, kvslc_p, kvwin_p, gates_p = in_projection(
        xp, row(norm1_g), mod_p[0], mod_p[1], w_in_bf, tabs_p, tm_p,
        lambda i: i // (t // tm_p), lambda i: i % (t // tm_p))
    osb_p = sb_prompt(qsb_p, kvsb_p, b, t)
    kvc_p = compress_prompt(kvcmp_p, pe, w1, w2, b, t)
    onsa_p = nsa_prompt(qnsa_p, kvc_p, kvslc_p, kvwin_p, gates_p, b, t)
    tm_o = 512 if t % 512 == 0 else t
    x1_p, h2_p = out_projection(osb_p, onsa_p, row(out_g_sb), row(out_g_nsa), w_out_bf, xp,
                                mod_p[2], row(norm2_g), mod_p[3], mod_p[4], tm_o,
                                lambda i: i // (t // tm_o))

    xs = x_sample.reshape(rs, d)
    tabs_s = _rope_tables(past + jnp.tile(jnp.arange(nq), bs))
    qsb_s, kvsb_s, qnsa_s, kvcmp_s, kvslc_s, kvwin_s, gates_s = in_projection(
        xs, row(norm1_g), mod_s[0], mod_s[1], w_in_bf, tabs_s, rs, lambda i: 0, lambda i: 0)
    osb_s = sb_sample(cache_sb, page_table, qsb_s.reshape(bs, nq, SB_W), kvsb_s.reshape(bs, nq, 2 * SB_W))
    kvcmp_s3 = kvcmp_s.reshape(bs, nq, KV_W)
    kvc_s = compress_sample(cache_cmp, page_table, kvcmp_s3, pe, w1, w2)
    pad8 = lambda a: jnp.pad(a.reshape(bs, nq, KV_W), ((0, 0), (0, -nq % SUBLANES), (0, 0)))
    qnsa_s3 = qnsa_s.reshape(bs, nq, NSA_W)
    st_win = state_win.reshape(bs, state_win.shape[1] * 2 * G_NSA, HEAD_DIM)
    ocmp_s, owin_s, selx = nsa_sample_select(qnsa_s3, kvc_s, st_win, pad8(kvwin_s), past)
    onsa_s = nsa_sample_slc(cache_slc, page_table, qnsa_s3, pad8(kvslc_s), selx, ocmp_s, owin_s,
                            gates_s.reshape(bs, nq, GATE_W))
    x1_s, h2_s = out_projection(osb_s.reshape(rs, SB_W), onsa_s.reshape(rs, NSA_W), row(out_g_sb),
                                row(out_g_nsa), w_out_bf, xs, mod_s[2], row(norm2_g), mod_s[3],
                                mod_s[4], rs, lambda i: 0)

    r = rp + rs
    r_pad = -(-r // PEER_TN) * PEER_TN
    h2 = jnp.pad(jnp.concatenate([h2_p, h2_s]), ((0, r_pad - r), (0, 0)))
    s = peer_scores(h2, wq_bf, k1_bf, k2_bf)
    r2p, e2p, cnt, e1 = peer_select(s)
    peer_t = peer_experts(h2, u_bf, vt_bf, r2p, e2p, cnt, e1)

    states_p = (kvsb_p.reshape(b, t, 2, H_SB, HEAD_DIM), kvcmp_p.reshape(b, t, 2, G_NSA, HEAD_DIM),
                kvslc_p.reshape(b, t, 2, G_NSA, HEAD_DIM),
                kvwin_p.reshape(b, t, 2, G_NSA, HEAD_DIM)[:, t - min(WINDOW, t):])
    kvwin_s5 = kvwin_s.reshape(bs, nq, 2, G_NSA, HEAD_DIM)
    states_s = (kvsb_s.reshape(bs, nq, 2, H_SB, HEAD_DIM), kvcmp_s.reshape(bs, nq, 2, G_NSA, HEAD_DIM),
                kvslc_s.reshape(bs, nq, 2, G_NSA, HEAD_DIM),
                jnp.concatenate([state_win, kvwin_s5], axis=1)[:, nq:])
    return (x1_p, mod_p[5], x1_s, mod_s[5], peer_t, rp), states_p, states_s


def kernel(x_prompt, x_sample, cache_sb, cache_cmp, cache_slc, state_win, page_table, c_prompt, c_sample, w_ada, b_ada, norm1_g, w_in, cmp_pe_k, cmp_w1_k, cmp_w2_k, cmp_pe_v, cmp_w1_v, cmp_w2_v, out_g_sb, out_g_nsa, w_out, norm2_g, peer_wq, peer_k1, peer_k2, peer_u, peer_v, final_g):
    depth = w_ada.shape[0]
    assert depth == 1, "single-layer trunk"
    b, t, d = x_prompt.shape
    bs, nq, _ = x_sample.shape
    (x1_p, gate_p, x1_s, gate_s, peer_t, rp), st_p, st_s = _layer(
        x_prompt, x_sample, cache_sb[0], cache_cmp[0], cache_slc[0], state_win[0], page_table,
        c_prompt, c_sample, w_ada[0], b_ada[0], norm1_g[0], w_in[0], cmp_pe_k[0], cmp_w1_k[0],
        cmp_w2_k[0], cmp_pe_v[0], cmp_w1_v[0], cmp_w2_v[0], out_g_sb[0], out_g_nsa[0], w_out[0],
        norm2_g[0], peer_wq[0], peer_k1[0], peer_k2[0], peer_u[0], peer_v[0])
    fg = final_g.reshape(1, d)
    tm_f = 256
    y_p = final_norm(x1_p, peer_t, 0, gate_p, fg, tm_f, lambda i: i // (t // tm_f)).reshape(b, t, d)
    rs = bs * nq
    y_s = final_norm(x1_s, peer_t, rp, gate_s, fg, rs, lambda i: 0).reshape(bs, nq, d)
    return (y_p, y_s, st_p[0][None], st_s[0][None], st_p[1][None], st_s[1][None],
            st_p[2][None], st_s[2][None], st_p[3][None], st_s[3][None])
```
